```python
import math
import jax, jax.numpy as jnp
from jax import lax
import numpy as np

D_MODEL = 1024
BATCH = 4
SEQ = 4096
DEPTH = 4

A_HEADS = 4
A_DK = 128
A_DV = 128
A_CHUNK = 64
A_W = A_HEADS * A_DK
A_V = A_HEADS * A_DV
B_HEADS = 8
B_GROUPS = 2
B_HD = 64
B_Q = B_HEADS * B_HD
B_KV = B_GROUPS * B_HD
CMP_LEN = 32
CMP_STRIDE = 16
CMP_HIDDEN = 256
SEL_LEN = 64
SEL_TOP = 16
N_LOCAL = 2
WINDOW = 512
Q_BLOCK = 128
C_HEADS = 8
C_HD = 64
C_W = C_HEADS * C_HD
C_DECAY_LORA = 64
C_AAA_LORA = 64
C_GATE_LORA = 128
LNX_EPS = 64e-5
REL_BUCKETS = 32
REL_MAX_DIST = 128
D_FF = -(-8 * D_MODEL // (3 * 256)) * 256
N_BRANCH = 3
BRANCH_W = 512

A_SPLITS = (A_W, A_W, A_V, A_V)
B_SPLITS = (B_Q, B_KV, B_KV, B_KV, B_KV, B_KV, B_KV, 3 * B_HEADS)
C_SPLITS = (C_W, C_W, C_W, C_DECAY_LORA, C_AAA_LORA, C_GATE_LORA)
A_IN = sum(A_SPLITS)
B_IN = sum(B_SPLITS)
C_IN = sum(C_SPLITS)
GATE_W = N_BRANCH * D_MODEL
IN_GROUPS = (A_IN, B_IN, C_IN, GATE_W)
N_IN = sum(IN_GROUPS)

kernel_name = "hybrid_hgrn2_nsa_rwkv7_trunk"


def split_cols(p, sizes):
    return jnp.split(p, [int(s) for s in np.cumsum(sizes)[:-1]], axis=-1)


def rmsnorm(x, w, eps=1e-6):
    xf = x.astype(jnp.float32)
    y = xf * lax.rsqrt(jnp.mean(xf * xf, axis=-1, keepdims=True) + eps)
    return (y * w.astype(jnp.float32)).astype(x.dtype)


def modulate(h, shift, scale):
    return h * (1 + scale) + shift


def masked_softmax(s, mask):
    s = jnp.where(mask, s.astype(jnp.float32), -jnp.inf)
    m = jnp.max(s, axis=-1, keepdims=True)
    m = jnp.where(jnp.isfinite(m), m, 0.0)
    e = jnp.exp(s - m)
    return e / jnp.maximum(jnp.sum(e, axis=-1, keepdims=True), 1e-30)


def t5_bucket(dist):
    n = jnp.maximum(dist, 0)
    max_exact = REL_BUCKETS // 2
    nf = jnp.maximum(n, 1).astype(jnp.float32)
    large = max_exact + (jnp.log(nf / max_exact) / math.log(REL_MAX_DIST / max_exact)
                         * (REL_BUCKETS - max_exact)).astype(jnp.int32)
    large = jnp.minimum(large, REL_BUCKETS - 1)
    return jnp.where(n < max_exact, n, large)


def hgrn2_mixer(q, f_raw, i, g, lb, norm_w):
    Bsz, T, _ = q.shape
    n_c = T // A_CHUNK
    q = jax.nn.silu(q.astype(jnp.float32))
    f_raw = f_raw.astype(jnp.float32)
    log_f = jnp.logaddexp(jnp.log(lb), jnp.log1p(-lb) + jax.nn.log_sigmoid(f_raw))
    k = (1 - lb) * jax.nn.sigmoid(-f_raw)

    def heads(t, d):
        return t.reshape(Bsz, n_c, A_CHUNK, A_HEADS, d).transpose(1, 0, 3, 2, 4)

    qh, kh = heads(q, A_DK), heads(k, A_DK)
    vh = heads(i.astype(jnp.float32), A_DV)
    bcum = jnp.cumsum(heads(log_f, A_DK), axis=3)
    tril = jnp.tril(jnp.ones((A_CHUNK, A_CHUNK), bool))[:, :, None]

    def step(S, xs):
        qc, kc, vc, bc = xs
        o_inter = jnp.einsum('bhtd,bhde->bhte', qc * jnp.exp(bc), S)
        diff = bc[:, :, :, None, :] - bc[:, :, None, :, :]
        decay = jnp.where(tril, jnp.exp(jnp.where(tril, diff, 0.0)), 0.0)
        att = jnp.einsum('bhtd,bhtsd,bhsd->bhts', qc, decay, kc)
        o = o_inter + jnp.einsum('bhts,bhse->bhte', att, vc)
        b_last = bc[:, :, -1:, :]
        S = jnp.exp(b_last[:, :, 0, :, None]) * S + jnp.einsum('bhsd,bhse->bhde', kc * jnp.exp(b_last - bc), vc)
        return S, o

    S0 = jnp.zeros((Bsz, A_HEADS, A_DK, A_DV), jnp.float32)
    _, o = lax.scan(step, S0, (qh, kh, vh, bcum))
    o = o.transpose(1, 0, 3, 2, 4).reshape(Bsz, T, A_HEADS, A_DV)
    o = o * lax.rsqrt(jnp.mean(o * o, axis=-1, keepdims=True) + 1e-5) * norm_w.reshape(A_HEADS, A_DV)
    return o.reshape(Bsz, T, A_V) * jax.nn.sigmoid(g.astype(jnp.float32))


def nsa_mixer(q, k_cmp, v_cmp, k_sel, v_sel, k_win, v_win, gate_raw,
              pe_k, w1_k, w2_k, pe_v, w1_v, w2_v, rel_bias):
    f32 = jnp.float32
    q, k_cmp, v_cmp, k_sel, v_sel, k_win, v_win, gate_raw = (
        t.astype(f32) for t in (q, k_cmp, v_cmp, k_sel, v_sel, k_win, v_win, gate_raw))
    Bsz, T, _ = q.shape
    J = B_HEADS // B_GROUPS
    n_q = T // Q_BLOCK
    n_cmp = (T - CMP_LEN) // CMP_STRIDE + 1
    n_sel = T // SEL_LEN
    n_top = min(SEL_TOP, n_sel)
    scale = B_HD ** -0.5

    def kv_heads(t):
        return t.reshape(Bsz, T, B_GROUPS, B_HD).transpose(0, 2, 1, 3)

    cmp_start = jnp.arange(n_cmp) * CMP_STRIDE
    cmp_idx = cmp_start[:, None] + jnp.arange(CMP_LEN)[None, :]
    cmp_end = cmp_start + CMP_LEN - 1

    def compress(t, pe, w1, w2):
        blk = kv_heads(t)[:, :, cmp_idx] + pe
        blk = blk.reshape(Bsz, B_GROUPS, n_cmp, CMP_LEN * B_HD)
        return jax.nn.silu(blk @ w1) @ w2

    kc = compress(k_cmp, pe_k, w1_k, w2_k)
    vc = compress(v_cmp, pe_v, w1_v, w2_v)
    ks = kv_heads(k_sel).reshape(Bsz, B_GROUPS, n_sel, SEL_LEN, B_HD)
    vs = kv_heads(v_sel).reshape(Bsz, B_GROUPS, n_sel, SEL_LEN, B_HD)
    pad = ((0, 0), (0, 0), (WINDOW, 0), (0, 0))
    kw = jnp.pad(kv_heads(k_win), pad)
    vw = jnp.pad(kv_heads(v_win), pad)
    sel_start = jnp.arange(n_sel) * SEL_LEN
    cover = ((cmp_start[:, None] < sel_start[None, :] + SEL_LEN)
             & (cmp_start[:, None] + CMP_LEN > sel_start[None, :])).astype(f32)
    bias_gj = rel_bias.reshape(REL_BUCKETS, B_GROUPS, J).transpose(1, 0, 2)
    bi = jnp.arange(Bsz)[:, None, None, None]
    gi = jnp.arange(B_GROUPS)[None, :, None, None]
    win_off = jnp.arange(Q_BLOCK + WINDOW) - WINDOW
    blk_pos = jnp.arange(SEL_LEN)
    blk_ids = jnp.arange(n_sel)

    def head_bias(dist):
        return rel_bias[t5_bucket(dist)].reshape(*dist.shape, B_GROUPS, J).transpose(2, 3, 0, 1)

    def block(args):
        qb, gb, t0 = args
        t = t0 + jnp.arange(Q_BLOCK)
        s = jnp.einsum('bgjqd,bgnd->bgjqn', qb, kc) * scale + head_bias(t[:, None] - cmp_end[None, :])
        p_c = masked_softmax(s, cmp_end[None, :] <= t[:, None])
        o_c = jnp.einsum('bgjqn,bgnd->bgjqd', p_c, vc)
        imp = jnp.einsum('bgjqn,nm->bgqm', p_c, cover)
        cur = t // SEL_LEN
        causal = blk_ids[None, :] <= cur[:, None]
        forced = (blk_ids[None, :] == 0) | ((blk_ids[None, :] > cur[:, None] - N_LOCAL) & causal)
        score = jnp.where(causal, jnp.where(forced, jnp.inf, imp), -jnp.inf)
        _, idx = lax.top_k(score, n_top)
        k_g = ks[bi, gi, idx].reshape(Bsz, B_GROUPS, Q_BLOCK, n_top * SEL_LEN, B_HD)
        v_g = vs[bi, gi, idx].reshape(Bsz, B_GROUPS, Q_BLOCK, n_top * SEL_LEN, B_HD)
        pos = (idx[..., None] * SEL_LEN + blk_pos).reshape(Bsz, B_GROUPS, Q_BLOCK, n_top * SEL_LEN)
        dist = t[:, None] - pos
        b_s = jnp.moveaxis(bias_gj[gi, t5_bucket(dist)], -1, 2)
        s = jnp.einsum('bgjqd,bgqkd->bgjqk', qb, k_g) * scale + b_s
        p_s = masked_softmax(s, (dist >= 0)[:, :, None])
        o_s = jnp.einsum('bgjqk,bgqkd->bgjqd', p_s, v_g)
        kwb = lax.dynamic_slice_in_dim(kw, t0, Q_BLOCK + WINDOW, axis=2)
        vwb = lax.dynamic_slice_in_dim(vw, t0, Q_BLOCK + WINDOW, axis=2)
        kpos = t0 + win_off
        dist = t[:, None] - kpos[None, :]
        mask = (dist >= 0) & (dist < WINDOW) & (kpos >= 0)[None, :]
        s = jnp.einsum('bgjqd,bgkd->bgjqk', qb, kwb) * scale + head_bias(dist)
        p_w = masked_softmax(s, mask)
        o_w = jnp.einsum('bgjqk,bgkd->bgjqd', p_w, vwb)
        g = jax.nn.sigmoid(gb)
        return g[..., 0:1] * o_c + g[..., 1:2] * o_s + g[..., 2:3] * o_w

    qh = q.reshape(Bsz, T, B_GROUPS, J, B_HD).transpose(0, 2, 3, 1, 4)
    q_blocks = jnp.moveaxis(qh.reshape(Bsz, B_GROUPS, J, n_q, Q_BLOCK, B_HD), 3, 0)
    gh = gate_raw.reshape(Bsz, T, B_GROUPS, J, 3).transpose(0, 2, 3, 1, 4)
    g_blocks = jnp.moveaxis(gh.reshape(Bsz, B_GROUPS, J, n_q, Q_BLOCK, 3), 3, 0)
    starts = jnp.arange(n_q, dtype=jnp.int32) * Q_BLOCK
    o = lax.map(block, (q_blocks, g_blocks, starts))
    return o.transpose(1, 0, 4, 2, 3, 5).reshape(Bsz, T, B_Q)


def rwkv7_mixer(p, mu, w0, w2, a0, a2, g2, k_k, k_a, r_k, lnx_w, lnx_b):
    Bsz, T, _ = p.shape
    p = p.astype(jnp.float32)
    p_prev = jnp.pad(p[:, :-1], ((0, 0), (1, 0), (0, 0)))
    p = p + (p_prev - p) * mu
    r, k, v, xw, xa, xg = split_cols(p, C_SPLITS)
    w = -jax.nn.softplus(-(w0 + jnp.tanh(xw) @ w2)) - 0.5
    decay = jnp.exp(-jnp.exp(w))
    a = jax.nn.sigmoid(a0 + xa @ a2)
    g = jax.nn.sigmoid(xg) @ g2

    def hd(t):
        return t.reshape(Bsz, T, C_HEADS, C_HD)

    kk = hd(k * k_k)
    kk = kk * lax.rsqrt(jnp.maximum(jnp.sum(kk * kk, axis=-1, keepdims=True), 1e-24))
    k = k * (1 + (a - 1) * k_a)
    r_h, k_h, v_h, a_h, w_h = hd(r), hd(k), hd(v), hd(a), hd(decay)

    def step(S, xs):
        r_t, w_t, k_t, v_t, kk_t, akk_t = xs
        sa = jnp.einsum('bhvk,bhk->bhv', S, kk_t)
        S = S * w_t[:, :, None, :] - sa[..., None] * akk_t[:, :, None, :] + v_t[..., None] * k_t[:, :, None, :]
        return S, jnp.einsum('bhvk,bhk->bhv', S, r_t)

    S0 = jnp.zeros((Bsz, C_HEADS, C_HD, C_HD), jnp.float32)
    xs = tuple(jnp.moveaxis(t, 1, 0) for t in (r_h, w_h, k_h, v_h, kk, a_h * kk))
    _, y = lax.scan(step, S0, xs)
    y = jnp.moveaxis(y, 0, 1)
    mean = jnp.mean(y, axis=-1, keepdims=True)
    var = jnp.mean(jnp.square(y - mean), axis=-1, keepdims=True)
    y = (y - mean) * lax.rsqrt(var + LNX_EPS) * lnx_w.reshape(C_HEADS, C_HD) + lnx_b.reshape(C_HEADS, C_HD)
    y = y + jnp.sum(r_h * k_h * r_k, axis=-1, keepdims=True) * v_h
    return y.reshape(Bsz, T, C_W) * g


def setup_inputs(seed: int = 0) -> dict:
    key = jax.random.key(seed)
    keys = iter(jax.random.split(key, 48))

    def nrm(shape, scale):
        return jax.random.normal(next(keys), shape, jnp.float32) * scale

    def gain(shape):
        return 1.0 + nrm(shape, 0.02)

    L = DEPTH
    return {
        "x": nrm((BATCH, SEQ, D_MODEL), 1.0),
        "c": nrm((BATCH, D_MODEL), 1.0),
        "ada_w": nrm((L, D_MODEL, 6 * D_MODEL), 0.5 * D_MODEL ** -0.5),
        "ada_b": nrm((L, 6 * D_MODEL), 0.02),
        "norm1_w": gain((L, D_MODEL)),
        "norm2_w": gain((L, D_MODEL)),
        "w_in": nrm((L, D_MODEL, N_IN), D_MODEL ** -0.5),
        "hgrn_lb_logits": nrm((L, A_W), 0.5),
        "hgrn_norm_w": gain((L, A_V)),
        "nsa_pe_k": nrm((L, CMP_LEN, B_HD), 0.1),
        "nsa_cmp_w1_k": nrm((L, CMP_LEN * B_HD, CMP_HIDDEN), (CMP_LEN * B_HD) ** -0.5),
        "nsa_cmp_w2_k": nrm((L, CMP_HIDDEN, B_HD), CMP_HIDDEN ** -0.5),
        "nsa_pe_v": nrm((L, CMP_LEN, B_HD), 0.1),
        "nsa_cmp_w1_v": nrm((L, CMP_LEN * B_HD, CMP_HIDDEN), (CMP_LEN * B_HD) ** -0.5),
        "nsa_cmp_w2_v": nrm((L, CMP_HIDDEN, B_HD), CMP_HIDDEN ** -0.5),
        "rel_bias": nrm((REL_BUCKETS, B_HEADS), 0.5),
        "rw_mu": jax.random.uniform(next(keys), (L, C_IN), jnp.float32),
        "rw_w0": nrm((L, C_W), 1.0),
        "rw_w2": nrm((L, C_DECAY_LORA, C_W), 0.1),
        "rw_a0": nrm((L, C_W), 0.5),
        "rw_a2": nrm((L, C_AAA_LORA, C_W), 0.1),
        "rw_g2": nrm((L, C_GATE_LORA, C_W), C_GATE_LORA ** -0.5),
        "rw_k_k": 1.0 + nrm((L, C_W), 0.1),
        "rw_k_a": 1.0 + nrm((L, C_W), 0.1),
        "rw_r_k": nrm((L, C_HEADS, C_HD), 0.1),
        "rw_lnx_w": gain((L, C_W)),
        "rw_lnx_b": nrm((L, C_W), 0.02),
        "w_branch": nrm((L, N_BRANCH, BRANCH_W, D_MODEL), BRANCH_W ** -0.5),
        "w_out": nrm((L, D_MODEL, D_MODEL), D_MODEL ** -0.5),
        "ffn_w1": nrm((L, D_MODEL, D_FF), D_MODEL ** -0.5),
        "ffn_w3": nrm((L, D_MODEL, D_FF), D_MODEL ** -0.5),
        "ffn_w2": nrm((L, D_FF, D_MODEL), D_FF ** -0.5),
        "final_norm_w": gain((D_MODEL,)),
    }


def reference(x, c, ada_w, ada_b, norm1_w, norm2_w, w_in, hgrn_lb_logits, hgrn_norm_w,
              nsa_pe_k, nsa_cmp_w1_k, nsa_cmp_w2_k, nsa_pe_v, nsa_cmp_w1_v, nsa_cmp_w2_v, rel_bias,
              rw_mu, rw_w0, rw_w2, rw_a0, rw_a2, rw_g2, rw_k_k, rw_k_a, rw_r_k, rw_lnx_w, rw_lnx_b,
              w_branch, w_out, ffn_w1, ffn_w3, ffn_w2, final_norm_w):
    dt = x.dtype
    lb = jnp.cumsum(jax.nn.softmax(hgrn_lb_logits.astype(jnp.float32), axis=0), axis=0)
    lb = lb - lb[0]
    cond = jax.nn.silu(c)
    for l in range(DEPTH):
        ada = (cond @ ada_w[l] + ada_b[l])[:, None, :]
        sh1, sc1, gt1, sh2, sc2, gt2 = jnp.split(ada, 6, axis=-1)
        h = modulate(rmsnorm(x, norm1_w[l]), sh1, sc1)
        p = h @ w_in[l]
        p_a, p_b, p_c, p_gate = split_cols(p, IN_GROUPS)
        y_a = hgrn2_mixer(*split_cols(p_a, A_SPLITS), lb[l], hgrn_norm_w[l])
        y_b = nsa_mixer(*split_cols(p_b, B_SPLITS), nsa_pe_k[l], nsa_cmp_w1_k[l], nsa_cmp_w2_k[l],
                        nsa_pe_v[l], nsa_cmp_w1_v[l], nsa_cmp_w2_v[l], rel_bias)
        y_c = rwkv7_mixer(p_c, rw_mu[l], rw_w0[l], rw_w2[l], rw_a0[l], rw_a2[l], rw_g2[l],
                          rw_k_k[l], rw_k_a[l], rw_r_k[l], rw_lnx_w[l], rw_lnx_b[l])
        ga, gb, gc = jnp.split(jax.nn.sigmoid(p_gate), 3, axis=-1)
        merged = (ga * (y_a.astype(dt) @ w_branch[l, 0])
                  + gb * (y_b.astype(dt) @ w_branch[l, 1])
                  + gc * (y_c.astype(dt) @ w_branch[l, 2]))
        x = x + gt1 * (merged @ w_out[l])
        h = modulate(rmsnorm(x, norm2_w[l]), sh2, sc2)
        x = x + gt2 * ((jax.nn.silu(h @ ffn_w1[l]) * (h @ ffn_w3[l])) @ ffn_w2[l])
    return rmsnorm(x, final_norm_w)
```

```python
import functools
import math

import jax
import jax.numpy as jnp
from jax import lax
from jax.experimental import pallas as pl
from jax.experimental.pallas import tpu as pltpu

F32 = jnp.float32
BF16 = jnp.bfloat16
I32 = jnp.int32

LANES = 128
VMEM_LIMIT = 56 * 1024 * 1024

D_MODEL = 1024
DEPTH = 4
A_HEADS, A_DK, A_DV = 4, 128, 128
A_CHUNK, A_SUB = 64, 16
B_HEADS, B_GROUPS, B_HD = 8, 2, 64
B_J = B_HEADS // B_GROUPS
CMP_LEN, CMP_STRIDE, CMP_HIDDEN = 32, 16, 256
SEL_LEN, SEL_TOP, N_LOCAL = 64, 16, 2
WINDOW, Q_BLOCK = 512, 128
N_WIN_TILES = WINDOW // Q_BLOCK + 1
REL_BUCKETS, REL_MAX_DIST = 32, 128
C_HEADS, C_HD = 8, 64
C_CHUNK = 64
C_DECAY_LORA, C_AAA_LORA, C_GATE_LORA = 64, 64, 128
LNX_EPS = 64e-5
D_FF = 2816

COL_A = 0
COL_GATE = 2048 // LANES
COL_C = 5120 // LANES
COL_B = 6912 // LANES
P_COLS = 8320


def _cparams(sem):
    return pltpu.CompilerParams(dimension_semantics=sem, vmem_limit_bytes=VMEM_LIMIT)


def _split3(x):
    hi = x.astype(BF16)
    r1 = x - hi.astype(F32)
    mid = r1.astype(BF16)
    lo = (r1 - mid.astype(F32)).astype(BF16)
    return hi, mid, lo


def _mm(a, b):
    return jnp.dot(a.astype(BF16), b.astype(BF16), preferred_element_type=F32)


def _mm_nt(a, b):
    return lax.dot_general(a.astype(BF16), b.astype(BF16), (((1,), (1,)), ((), ())),
                           preferred_element_type=F32)


def _mm_tn(a, b):
    return lax.dot_general(a.astype(BF16), b.astype(BF16), (((0,), (0,)), ((), ())),
                           preferred_element_type=F32)


def _mm_exact_lhs(a_exact, b):
    a = a_exact.astype(BF16)
    hi, mid, lo = _split3(b)
    d = functools.partial(jnp.dot, preferred_element_type=F32)
    return d(a, hi) + d(a, mid) + d(a, lo)


def _mm_exact_rhs(a, b_exact):
    b = b_exact.astype(BF16)
    hi, mid, lo = _split3(a)
    d = functools.partial(jnp.dot, preferred_element_type=F32)
    return d(hi, b) + d(mid, b) + d(lo, b)


def _mm_x3(a, b):
    ah = a.astype(BF16)
    al = (a - ah.astype(F32)).astype(BF16)
    bh = b.astype(BF16)
    bl = (b - bh.astype(F32)).astype(BF16)
    d = functools.partial(jnp.dot, preferred_element_type=F32)
    return d(ah, bh) + d(ah, bl) + d(al, bh)


def _sigmoid(x):
    return 1.0 / (1.0 + jnp.exp(-x))


def _silu(x):
    return x * _sigmoid(x)


def _log_sigmoid(x):
    return jnp.minimum(x, 0.0) - jnp.log(1.0 + jnp.exp(-jnp.abs(x)))


def _iota(shape, dim):
    return lax.broadcasted_iota(I32, shape, dim)


def _ada_kernel(c_ref, w_ref, b_ref, o_ref):
    c = c_ref[...]
    o_ref[0] = _mm_x3(_silu(c), w_ref[0]) + b_ref[0]


def _ada(c_pad, ada_w, ada_b):
    L = ada_w.shape[0]
    n_out = ada_w.shape[2]
    tn = D_MODEL
    return pl.pallas_call(
        _ada_kernel,
        grid=(L, n_out // tn),
        in_specs=[pl.BlockSpec((8, D_MODEL), lambda l, j: (0, 0)),
                  pl.BlockSpec((1, D_MODEL, tn), lambda l, j: (l, 0, j)),
                  pl.BlockSpec((1, 1, tn), lambda l, j: (l, 0, j))],
        out_specs=pl.BlockSpec((1, 8, tn), lambda l, j: (l, 0, j)),
        out_shape=jax.ShapeDtypeStruct((L, 8, n_out), F32),
        compiler_params=_cparams(("arbitrary", "arbitrary")),
        name="ada_ln",
    )(c_pad, ada_w, ada_b.reshape(L, 1, n_out))


def _norm_mod(x, nw, sh, sc):
    y = x * lax.rsqrt(jnp.mean(x * x, axis=-1, keepdims=True) + 1e-6) * nw
    return y * (1.0 + sc) + sh


def _inproj_kernel(x_ref, nw_ref, sh_ref, sc_ref, w_ref, o_ref, h_scr):
    @pl.when(pl.program_id(2) == 0)
    def _():
        h_scr[...] = _norm_mod(x_ref[0], nw_ref[...], sh_ref[0], sc_ref[0]).astype(BF16)

    o_ref[0] = jnp.dot(h_scr[...], w_ref[...], preferred_element_type=F32)


def _inproj(x, nw, sh, sc, w):
    Bsz, T, D = x.shape
    n_out = w.shape[1]
    tm, tn = 512, 640
    return pl.pallas_call(
        _inproj_kernel,
        grid=(Bsz, T // tm, n_out // tn),
        in_specs=[pl.BlockSpec((1, tm, D), lambda b, i, n: (b, i, 0)),
                  pl.BlockSpec((1, D), lambda b, i, n: (0, 0)),
                  pl.BlockSpec((1, 1, D), lambda b, i, n: (b, 0, 0)),
                  pl.BlockSpec((1, 1, D), lambda b, i, n: (b, 0, 0)),
                  pl.BlockSpec((D, tn), lambda b, i, n: (0, n))],
        out_specs=pl.BlockSpec((1, tm, tn), lambda b, i, n: (b, i, n)),
        out_shape=jax.ShapeDtypeStruct((Bsz, T, n_out), F32),
        scratch_shapes=[pltpu.VMEM((tm, D), BF16)],
        compiler_params=_cparams(("arbitrary", "arbitrary", "arbitrary")),
        name="in_proj",
    )(x, nw, sh, sc, w)


def _hgrn_kernel(layer, lbl_ref, nw_ref, q_ref, f_ref, i_ref, g_ref, o_ref, s_scr):
    C, c = A_CHUNK, A_SUB

    @pl.when(pl.program_id(2) == 0)
    def _():
        s_scr[...] = jnp.zeros_like(s_scr)

    logits = lbl_ref[...]
    ex = jnp.exp(logits - jnp.max(logits, axis=0, keepdims=True))
    sm = ex / jnp.sum(ex, axis=0, keepdims=True)
    lb = jnp.zeros((1, A_DK), F32)
    for i in range(1, layer + 1):
        lb = lb + sm[i:i + 1, :]

    q = _silu(q_ref[0])
    fr = f_ref[0]
    v = i_ref[0]
    la = jnp.log(lb)
    lc = jnp.log(1.0 - lb) + _log_sigmoid(fr)
    log_f = jnp.maximum(la, lc) + jnp.log(1.0 + jnp.exp(-jnp.abs(la - lc)))
    k = (1.0 - lb) * _sigmoid(-fr)

    tri = (_iota((C, C), 0) >= _iota((C, C), 1)).astype(BF16)
    b = _mm_exact_lhs(tri, log_f)
    b_last = b[C - 1:C, :]
    st = s_scr[...]

    o = _mm_nt(q * jnp.exp(b), st)
    off = [jnp.zeros((c, A_DV), F32)]
    for blk in range(1, C // c):
        lo = blk * c
        ref = b[lo - 1:lo, :]
        qi = q[lo:lo + c] * jnp.exp(b[lo:lo + c] - ref)
        ki = k[:lo] * jnp.exp(ref - b[:lo])
        off.append(_mm(_mm_nt(qi, ki), v[:lo]))
    o = o + jnp.concatenate(off, axis=0)
    row = _iota((C, 1), 0) % c
    for j in range(c):
        if j == 0:
            kj, bj, vj = k, b, v
        else:
            kj, bj, vj = (pltpu.roll(t, j, 0) for t in (k, b, v))
        ok = row >= j
        e = jnp.exp(jnp.where(ok, b - bj, 0.0))
        a = jnp.sum(jnp.where(ok, q * kj * e, 0.0), axis=1, keepdims=True)
        o = o + a * vj

    s_scr[...] = st * jnp.exp(b_last) + _mm_tn(v, k * jnp.exp(b_last - b))

    o = o * lax.rsqrt(jnp.mean(o * o, axis=-1, keepdims=True) + 1e-5) * nw_ref[...]
    o_ref[0] = o * _sigmoid(g_ref[0])


def _hgrn(p, lb_logits, norm_w, layer):
    Bsz, T, _ = p.shape
    C = A_CHUNK
    L = lb_logits.shape[0]

    def col(off):
        return pl.BlockSpec((1, C, A_DK), lambda b, h, c: (b, c, COL_A + off + h))

    return pl.pallas_call(
        functools.partial(_hgrn_kernel, layer),
        grid=(Bsz, A_HEADS, T // C),
        in_specs=[pl.BlockSpec((L, A_DK), lambda b, h, c: (0, h)),
                  pl.BlockSpec((1, A_DV), lambda b, h, c: (0, h)),
                  col(0), col(A_HEADS), col(2 * A_HEADS), col(3 * A_HEADS)],
        out_specs=pl.BlockSpec((1, C, A_DV), lambda b, h, c: (b, c, h)),
        out_shape=jax.ShapeDtypeStruct((Bsz, T, A_HEADS * A_DV), F32),
        scratch_shapes=[pltpu.VMEM((A_DV, A_DK), F32)],
        compiler_params=_cparams(("arbitrary", "arbitrary", "arbitrary")),
        name="hgrn2",
    )(lb_logits, norm_w.reshape(1, -1), p, p, p, p)


def _inv_unit_lower(a):
    C = a.shape[0]
    eye = (_iota((C, C), 0) == _iota((C, C), 1)).astype(F32)
    n = -a
    x = eye + n
    p = 2
    while p < C:
        n = _mm_x3(n, n)
        x = x + _mm_x3(x, n)
        p *= 2
    return x


def _rwkv_kernel(r_ref, k_ref, v_ref, wa_ref, xg_ref,
                 mur_ref, muk_ref, muv_ref, muwa_ref, mug_ref,
                 w2_ref, a2_ref, g2_ref, w0_ref, a0_ref, kk_ref, ka_ref, rk_ref, lnw_ref, lnb_ref,
                 o_ref, s_scr, pr_scr, pk_scr, pv_scr, pwa_scr, pg_scr):
    C = C_CHUNK

    @pl.when(pl.program_id(2) == 0)
    def _():
        s_scr[...] = jnp.zeros_like(s_scr)
        for scr in (pr_scr, pk_scr, pv_scr, pwa_scr, pg_scr):
            scr[...] = jnp.zeros_like(scr)

    row0 = _iota((C, 1), 0) == 0

    def shifted(ref, prev_scr, mu_ref):
        cur = ref[0]
        prev = jnp.where(row0, prev_scr[...], pltpu.roll(cur, 1, 0))
        prev_scr[...] = cur[C - 1:C, :]
        return cur + (prev - cur) * mu_ref[...]

    r = shifted(r_ref, pr_scr, mur_ref)
    k = shifted(k_ref, pk_scr, muk_ref)
    v = shifted(v_ref, pv_scr, muv_ref)
    xwa = shifted(wa_ref, pwa_scr, muwa_ref)
    xg = shifted(xg_ref, pg_scr, mug_ref)

    lane = _iota((1, LANES), 1)
    head0 = lane < C_HD
    bd = ((_iota((LANES, LANES), 0) < C_HD) == (_iota((LANES, LANES), 1) < C_HD))
    bd_f = bd.astype(F32)

    w = w0_ref[...] + _mm_x3(jnp.tanh(xwa), w2_ref[...])
    sp = jnp.maximum(-w, 0.0) + jnp.log(1.0 + jnp.exp(-jnp.abs(w)))
    lw = -jnp.exp(-sp - 0.5)
    a = _sigmoid(a0_ref[...] + _mm_x3(xwa, a2_ref[...]))
    g = _mm_x3(_sigmoid(xg), g2_ref[...])

    kk = k * kk_ref[...]
    kk = kk * lax.rsqrt(jnp.maximum(_mm_exact_rhs(kk * kk, bd_f), 1e-24))
    k = k * (1.0 + (a - 1.0) * ka_ref[...])
    al = a * kk

    tri_incl = _iota((C, C), 0) >= _iota((C, C), 1)
    tri_strict = _iota((C, C), 0) > _iota((C, C), 1)
    b = _mm_exact_lhs(tri_incl.astype(BF16), lw)
    b_last = b[C - 1:C, :]
    inv_p = jnp.exp(-b)
    kap = kk * jnp.exp(b - lw)
    al_h = al * inv_p
    k_h = k * inv_p
    r_t = r * jnp.exp(b)

    s = s_scr[...]
    kap_s = _mm_nt(kap, s)
    r_s = _mm_nt(r_t, s)

    def per_head(fn):
        return jnp.where(head0, fn(head0), fn(jnp.logical_not(head0)))

    def solve(mask):
        kap_m = jnp.where(mask, kap, 0.0)
        a_ak = jnp.where(tri_strict, _mm_nt(kap_m, al_h), 0.0)
        a_kk = jnp.where(tri_strict, _mm_nt(kap_m, k_h), 0.0)
        t_inv = _inv_unit_lower(a_ak)
        return -_mm_x3(t_inv, kap_s + _mm(a_kk, v))

    u = per_head(solve)

    def readout(mask):
        r_m = jnp.where(mask, r_t, 0.0)
        b_al = jnp.where(tri_incl, _mm_nt(r_m, al_h), 0.0)
        b_k = jnp.where(tri_incl, _mm_nt(r_m, k_h), 0.0)
        return _mm(b_al, u) + _mm(b_k, v)

    y = r_s + per_head(readout)

    e_end = jnp.exp(b_last - b)
    s_new = s * jnp.exp(b_last) + _mm_tn(u, al * e_end) + _mm_tn(v, k * e_end)
    s_scr[...] = jnp.where(bd, s_new, 0.0)

    inv_hd = 1.0 / C_HD
    mean = _mm_exact_rhs(y, bd_f) * inv_hd
    yc = y - mean
    var = _mm_exact_rhs(yc * yc, bd_f) * inv_hd
    y = yc * lax.rsqrt(var + LNX_EPS) * lnw_ref[...] + lnb_ref[...]
    y = y + _mm_exact_rhs(r * k * rk_ref[...], bd_f) * v
    o_ref[0] = y * g


def _rwkv(p, mu, w2p, a2p, g2, w0, a0, k_k, k_a, r_k, lnx_w, lnx_b):
    Bsz, T, _ = p.shape
    C = C_CHUNK
    n_hp = C_HEADS * C_HD // LANES

    def col(off, per_head=True):
        if per_head:
            return pl.BlockSpec((1, C, LANES), lambda b, h, c: (b, c, COL_C + off + h))
        return pl.BlockSpec((1, C, LANES), lambda b, h, c: (b, c, COL_C + off))

    def vec(off, per_head=True):
        if per_head:
            return pl.BlockSpec((1, LANES), lambda b, h, c: (0, off + h))
        return pl.BlockSpec((1, LANES), lambda b, h, c: (0, off))

    def wcol(rows):
        return pl.BlockSpec((rows, LANES), lambda b, h, c: (0, h))

    row = lambda t: t.reshape(1, -1)
    return pl.pallas_call(
        _rwkv_kernel,
        grid=(Bsz, n_hp, T // C),
        in_specs=[col(0), col(n_hp), col(2 * n_hp), col(3 * n_hp, False), col(3 * n_hp + 1, False),
                  vec(0), vec(n_hp), vec(2 * n_hp), vec(3 * n_hp, False), vec(3 * n_hp + 1, False),
                  wcol(LANES), wcol(LANES), wcol(C_GATE_LORA),
                  vec(0), vec(0), vec(0), vec(0), vec(0), vec(0), vec(0)],
        out_specs=pl.BlockSpec((1, C, LANES), lambda b, h, c: (b, c, h)),
        out_shape=jax.ShapeDtypeStruct((Bsz, T, C_HEADS * C_HD), F32),
        scratch_shapes=[pltpu.VMEM((LANES, LANES), F32)] + [pltpu.VMEM((1, LANES), F32)] * 5,
        compiler_params=_cparams(("arbitrary", "arbitrary", "arbitrary")),
        name="rwkv7",
    )(p, p, p, p, p, row(mu), row(mu), row(mu), row(mu), row(mu),
      w2p, a2p, g2, row(w0), row(a0), row(k_k), row(k_a), row(r_k), row(lnx_w), row(lnx_b))


def _t5_thresholds():
    max_exact = REL_BUCKETS // 2
    out = []
    for kk in range(1, REL_BUCKETS - max_exact):
        n = max_exact
        while int(math.log(n / max_exact) / math.log(REL_MAX_DIST / max_exact) * (REL_BUCKETS - max_exact)) < kk:
            n += 1
        out.append(n)
    return out


def _bias_from_dist(dist, rb_ref, head):
    n = jnp.maximum(dist, 0)
    max_exact = REL_BUCKETS // 2
    large = jnp.full(n.shape, max_exact, I32)
    for thr in _t5_thresholds():
        large = large + (n >= thr).astype(I32)
    bucket = jnp.where(n < max_exact, n, large)
    out = jnp.zeros(n.shape, F32)
    for j in range(REL_BUCKETS):
        out = jnp.where(bucket == j, rb_ref[j, head], out)
    return out


def _bias_cmp_kernel(rb_ref, o_ref):
    tq, n_pad = o_ref.shape[1], o_ref.shape[2]
    t = pl.program_id(0) * tq + _iota((tq, n_pad), 0)
    cmp_end = _iota((tq, n_pad), 1) * CMP_STRIDE + CMP_LEN - 1
    for h in range(B_HEADS):
        o_ref[h] = _bias_from_dist(t - cmp_end, rb_ref, h)


def _bias_win_kernel(rb_ref, o_ref):
    d = pl.program_id(0) * Q_BLOCK + _iota((Q_BLOCK, Q_BLOCK), 0) - _iota((Q_BLOCK, Q_BLOCK), 1)
    for h in range(B_HEADS):
        o_ref[h, 0] = _bias_from_dist(d, rb_ref, h)


def _bias_tables(rel_bias, T, n_pad):
    smem = pl.BlockSpec(memory_space=pltpu.SMEM)
    bias_c = pl.pallas_call(
        _bias_cmp_kernel,
        grid=(T // Q_BLOCK,),
        in_specs=[smem],
        out_specs=pl.BlockSpec((B_HEADS, Q_BLOCK, n_pad), lambda i: (0, i, 0)),
        out_shape=jax.ShapeDtypeStruct((B_HEADS, T, n_pad), F32),
        compiler_params=_cparams(("arbitrary",)),
        name="nsa_bias_cmp",
    )(rel_bias)
    bias_w = pl.pallas_call(
        _bias_win_kernel,
        grid=(N_WIN_TILES,),
        in_specs=[smem],
        out_specs=pl.BlockSpec((B_HEADS, 1, Q_BLOCK, Q_BLOCK), lambda i: (0, i, 0, 0)),
        out_shape=jax.ShapeDtypeStruct((B_HEADS, N_WIN_TILES, Q_BLOCK, Q_BLOCK), F32),
        compiler_params=_cparams(("arbitrary",)),
        name="nsa_bias_win",
    )(rel_bias)
    return bias_c, bias_w


def _compress_kernel(x_ref, pe_ref, w1_ref, w2_ref, o_ref):
    x = x_ref[0, 0, 0]
    half = CMP_STRIDE * B_HD
    w1a = w1_ref[0, :half, :]
    w1b = w1_ref[0, half:, :]
    pe = pe_ref[0]
    n_grp = x.shape[0]
    hb = _mm(x, w1b)
    h = _mm(x, w1a) + pltpu.roll(hb, n_grp - 1, 0) + _mm(pe[:, :half], w1a) + _mm(pe[:, half:], w1b)
    out = _mm(_silu(h), w2_ref[0])
    o_ref[0, 0, 0] = jnp.where(_iota(out.shape, 0) < n_grp - 1, out, 0.0)


def _compress(src, pe, w1, w2):
    _, Bsz, G, n_grp, width = src.shape
    return pl.pallas_call(
        _compress_kernel,
        grid=(2, Bsz, G),
        in_specs=[pl.BlockSpec((1, 1, 1, n_grp, width), lambda s, b, g: (s, b, g, 0, 0)),
                  pl.BlockSpec((1, 1, CMP_LEN * B_HD), lambda s, b, g: (s, 0, 0)),
                  pl.BlockSpec((1, CMP_LEN * B_HD, CMP_HIDDEN), lambda s, b, g: (s, 0, 0)),
                  pl.BlockSpec((1, CMP_HIDDEN, B_HD), lambda s, b, g: (s, 0, 0))],
        out_specs=pl.BlockSpec((1, 1, 1, n_grp, B_HD), lambda s, b, g: (s, b, g, 0, 0)),
        out_shape=jax.ShapeDtypeStruct((2, Bsz, G, n_grp, B_HD), F32),
        compiler_params=_cparams(("arbitrary", "arbitrary", "arbitrary")),
        name="nsa_compress",
    )(src, pe, w1, w2)


def _nsa_kernel(q_ref, gate_ref, kc_ref, vc_ref, ks_ref, vs_ref, kw_ref, vw_ref, bc_ref, bw_ref, cov_ref,
                o_ref):
    QB, J, HD = Q_BLOCK, B_J, B_HD
    qb = pl.program_id(2)
    t0 = qb * QB
    n_pad = kc_ref.shape[3]
    NEG = -jnp.inf

    q = (q_ref[0, 0] * (HD ** -0.5)).reshape(J * QB, HD).astype(BF16)
    t = t0 + _iota((QB, 1), 0)

    def finish(e_sum):
        return 1.0 / jnp.maximum(e_sum, 1e-30)

    n_i = _iota((1, n_pad), 1)
    mask_c = ((n_i * CMP_STRIDE + CMP_LEN - 1) <= t) & (n_i < n_pad - 1)
    s = _mm_nt(q, kc_ref[0, 0, 0]).reshape(J, QB, n_pad) + bc_ref[...]
    s = jnp.where(mask_c[None], s, NEG)
    m = jnp.max(s, axis=-1, keepdims=True)
    m = jnp.where(m == NEG, 0.0, m)
    e = jnp.exp(s - m)
    p_c = e * finish(jnp.sum(e, axis=-1, keepdims=True))
    o_c = _mm(p_c.reshape(J * QB, n_pad), vc_ref[0, 0, 0]).reshape(J, QB, HD)

    p_sum = p_c[0]
    for j in range(1, J):
        p_sum = p_sum + p_c[j]
    ph = p_sum.astype(BF16)
    pl_ = (p_sum - ph.astype(F32)).astype(BF16)
    cov = cov_ref[...]
    imp = jnp.dot(ph, cov, preferred_element_type=F32) + jnp.dot(pl_, cov, preferred_element_type=F32)
    n_blk = cov.shape[1]
    m_i = _iota((1, n_blk), 1)
    cur = t // SEL_LEN
    causal = m_i <= cur
    forced = (m_i == 0) | ((m_i > cur - N_LOCAL) & causal)
    score = jnp.where(causal, jnp.where(forced, jnp.inf, imp), NEG)
    n_real = ks_ref.shape[2] // SEL_LEN
    st = score.T[:n_real, :]
    blk = _iota((n_real, 1), 0)
    cnt = jnp.zeros((n_real, QB), F32)
    for mp in range(n_real):
        other = st[mp:mp + 1, :]
        beats = (other > st) | ((other == st) & (mp < blk))
        cnt = cnt + beats.astype(F32)
    sel_t = (cnt < float(SEL_TOP)).astype(F32)
    if n_real < n_blk:
        sel_t = jnp.concatenate([sel_t, jnp.zeros((n_blk - n_real, QB), F32)], axis=0)
    sel = sel_t.T.astype(BF16)

    key_in_tile = _iota((1, QB), 1)
    blk_of_key = key_in_tile // SEL_LEN
    blk_row = _iota((n_blk, 1), 0)
    blocks_per_tile = QB // SEL_LEN

    def sel_step(kt, carry):
        m_old, l_old, acc = carry
        start = pl.multiple_of(kt * QB, QB)
        k_t = ks_ref[0, 0, pl.ds(start, QB), :]
        v_t = vs_ref[0, 0, pl.ds(start, QB), :]
        expand = ((blk_row - kt * blocks_per_tile) == blk_of_key).astype(BF16)
        member = jnp.dot(sel, expand, preferred_element_type=F32) > 0.5
        ok = member & ((kt * QB + key_in_tile) <= t)
        sc = _mm_nt(q, k_t).reshape(J, QB, QB) + bw_ref[:, jnp.minimum(qb - kt, 2)]
        sc = jnp.where(ok[None], sc, NEG)
        m_new = jnp.maximum(m_old, jnp.max(sc, axis=-1, keepdims=True))
        m_use = jnp.where(m_new == NEG, 0.0, m_new)
        pr = jnp.exp(sc - m_use)
        alpha = jnp.exp(m_old - m_use)
        l_new = alpha * l_old + jnp.sum(pr, axis=-1, keepdims=True)
        pv = _mm(pr.reshape(J * QB, QB), v_t).reshape(J, QB, HD)
        return m_new, l_new, alpha * acc + pv

    init = (jnp.full((J, QB, 1), NEG, F32), jnp.zeros((J, QB, 1), F32), jnp.zeros((J, QB, HD), F32))
    _, l_s, acc_s = lax.fori_loop(0, qb + 1, sel_step, init)
    o_s = acc_s * finish(l_s)

    ii = _iota((QB, QB), 0)
    jj = _iota((QB, QB), 1)
    scores, values = [], []
    for d in range(N_WIN_TILES):
        kt = jnp.maximum(qb - d, 0)
        start = pl.multiple_of(kt * QB, QB)
        sc = _mm_nt(q, kw_ref[0, 0, pl.ds(start, QB), :]).reshape(J, QB, QB) + bw_ref[:, d]
        ok = qb >= d
        if d == 0:
            ok = ok & (jj <= ii)
        elif d == N_WIN_TILES - 1:
            ok = ok & (jj > ii)
        else:
            ok = jnp.broadcast_to(ok, (QB, QB))
        scores.append(jnp.where(ok[None], sc, NEG))
        values.append(vw_ref[0, 0, pl.ds(start, QB), :])
    m = functools.reduce(jnp.maximum, [jnp.max(sc, axis=-1, keepdims=True) for sc in scores])
    l_w = jnp.zeros((J, QB, 1), F32)
    acc_w = jnp.zeros((J, QB, HD), F32)
    for sc, v_t in zip(scores, values):
        pr = jnp.exp(sc - m)
        l_w = l_w + jnp.sum(pr, axis=-1, keepdims=True)
        acc_w = acc_w + _mm(pr.reshape(J * QB, QB), v_t).reshape(J, QB, HD)
    o_w = acc_w * finish(l_w)

    g = _sigmoid(gate_ref[0, 0])
    o_ref[0, 0] = g[:, :, 0:1] * o_c + g[:, :, 1:2] * o_s + g[:, :, 2:3] * o_w


def _nsa(q, gates, kvc, ks, vs, kw, vw, bias_c, bias_w, cover):
    Bsz, G, J, T, HD = q.shape
    n_pad = kvc.shape[3]
    QB = Q_BLOCK
    qspec = pl.BlockSpec((1, 1, J, QB, HD), lambda b, g, i: (b, g, 0, i, 0))
    full = pl.BlockSpec((1, 1, T, HD), lambda b, g, i: (b, g, 0, 0))
    return pl.pallas_call(
        _nsa_kernel,
        grid=(Bsz, G, T // QB),
        in_specs=[qspec,
                  pl.BlockSpec((1, 1, J, QB, 3), lambda b, g, i: (b, g, 0, i, 0)),
                  pl.BlockSpec((1, 1, 1, n_pad, HD), lambda b, g, i: (0, b, g, 0, 0)),
                  pl.BlockSpec((1, 1, 1, n_pad, HD), lambda b, g, i: (1, b, g, 0, 0)),
                  full, full, full, full,
                  pl.BlockSpec((J, QB, n_pad), lambda b, g, i: (g, i, 0)),
                  pl.BlockSpec((J, N_WIN_TILES, QB, QB), lambda b, g, i: (g, 0, 0, 0)),
                  pl.BlockSpec(cover.shape, lambda b, g, i: (0, 0))],
        out_specs=qspec,
        out_shape=jax.ShapeDtypeStruct(q.shape, F32),
        compiler_params=_cparams(("arbitrary", "arbitrary", "arbitrary")),
        name="nsa_attention",
    )(q, gates, kvc, kvc, ks, vs, kw, vw, bias_c, bias_w, cover)


def _merge_kernel(ya_ref, yb_ref, yc_ref, ga_ref, gb_ref, gc_ref, x_ref, gt_ref, wb_ref, wo_ref, o_ref):
    merged = (_sigmoid(ga_ref[0]) * _mm(ya_ref[0], wb_ref[0])
              + _sigmoid(gb_ref[0]) * _mm(yb_ref[0], wb_ref[1])
              + _sigmoid(gc_ref[0]) * _mm(yc_ref[0], wb_ref[2]))
    o_ref[0] = x_ref[0] + gt_ref[0] * _mm(merged, wo_ref[...])


def _merge(ya, yb, yc, p, x, gt, wb, wo):
    Bsz, T, D = x.shape
    tm = 512
    bw = ya.shape[2]
    gate_blk = COL_GATE * LANES // D
    yspec = pl.BlockSpec((1, tm, bw), lambda b, i: (b, i, 0))

    def gspec(k):
        return pl.BlockSpec((1, tm, D), lambda b, i: (b, i, gate_blk + k))

    return pl.pallas_call(
        _merge_kernel,
        grid=(Bsz, T // tm),
        in_specs=[yspec, yspec, yspec, gspec(0), gspec(1), gspec(2),
                  pl.BlockSpec((1, tm, D), lambda b, i: (b, i, 0)),
                  pl.BlockSpec((1, 1, D), lambda b, i: (b, 0, 0)),
                  pl.BlockSpec(wb.shape, lambda b, i: (0, 0, 0)),
                  pl.BlockSpec(wo.shape, lambda b, i: (0, 0))],
        out_specs=pl.BlockSpec((1, tm, D), lambda b, i: (b, i, 0)),
        out_shape=jax.ShapeDtypeStruct(x.shape, F32),
        compiler_params=_cparams(("arbitrary", "arbitrary")),
        name="merge_out_proj",
    )(ya, yb, yc, p, p, p, x, gt, wb, wo)


def _ffn_kernel(final, x_ref, nw_ref, sh_ref, sc_ref, gt_ref, w1_ref, w3_ref, w2_ref, fw_ref, o_ref,
                h_scr, acc_scr):
    f = pl.program_id(2)

    @pl.when(f == 0)
    def _():
        h_scr[...] = _norm_mod(x_ref[0], nw_ref[...], sh_ref[0], sc_ref[0]).astype(BF16)
        acc_scr[...] = jnp.zeros_like(acc_scr)

    h = h_scr[...]
    u = _silu(jnp.dot(h, w1_ref[...], preferred_element_type=F32)) * jnp.dot(h, w3_ref[...],
                                                                           preferred_element_type=F32)
    acc_scr[...] += _mm(u, w2_ref[...])

    @pl.when(f == pl.num_programs(2) - 1)
    def _():
        x = x_ref[0] + gt_ref[0] * acc_scr[...]
        if final:
            x = x * lax.rsqrt(jnp.mean(x * x, axis=-1, keepdims=True) + 1e-6) * fw_ref[...]
        o_ref[0] = x


def _ffn(x, nw, sh, sc, gt, w1, w3, w2, fw, final):
    Bsz, T, D = x.shape
    tm, tf = 512, 256
    n_ff = w1.shape[1]
    vec = pl.BlockSpec((1, D), lambda b, i, f: (0, 0))
    bvec = pl.BlockSpec((1, 1, D), lambda b, i, f: (b, 0, 0))
    xspec = pl.BlockSpec((1, tm, D), lambda b, i, f: (b, i, 0))
    return pl.pallas_call(
        functools.partial(_ffn_kernel, final),
        grid=(Bsz, T // tm, n_ff // tf),
        in_specs=[xspec, vec, bvec, bvec, bvec,
                  pl.BlockSpec((D, tf), lambda b, i, f: (0, f)),
                  pl.BlockSpec((D, tf), lambda b, i, f: (0, f)),
                  pl.BlockSpec((tf, D), lambda b, i, f: (f, 0)),
                  vec],
        out_specs=xspec,
        out_shape=jax.ShapeDtypeStruct(x.shape, F32),
        scratch_shapes=[pltpu.VMEM((tm, D), BF16), pltpu.VMEM((tm, D), F32)],
        compiler_params=_cparams(("arbitrary", "arbitrary", "arbitrary")),
        name="ffn_swiglu",
    )(x, nw, sh, sc, gt, w1, w3, w2, fw)


def _pack_w_in(w):
    a_in, b_in, c_in = 2048, 1304, 1792
    wa = w[:, :a_in]
    wb = w[:, a_in:a_in + b_in]
    wc = w[:, a_in + b_in:a_in + b_in + c_in]
    wg = w[:, a_in + b_in + c_in:]
    wb = jnp.pad(wb, ((0, 0), (0, P_COLS - COL_B * LANES - b_in)))
    return jnp.concatenate([wa, wg, wc, wb], axis=1).astype(BF16)


def _nsa_layer(p, pe_k, w1_k, w2_k, pe_v, w1_v, w2_v, bias_c, bias_w, cover):
    Bsz, T, _ = p.shape
    G, J, HD = B_GROUPS, B_J, B_HD
    base = COL_B * LANES
    pb = p[:, :, base:]
    q = pb[:, :, :G * J * HD].reshape(Bsz, T, G, J, HD).transpose(0, 2, 3, 1, 4)
    off = G * J * HD

    def kv(i):
        return pb[:, :, off + i * G * HD: off + (i + 1) * G * HD].reshape(Bsz, T, G, HD).transpose(0, 2, 1, 3)

    k_cmp, v_cmp, k_sel, v_sel, k_win, v_win = (kv(i) for i in range(6))
    goff = off + 6 * G * HD
    gates = pb[:, :, goff:goff + 3 * G * J].reshape(Bsz, T, G, J, 3).transpose(0, 2, 3, 1, 4)
    n_grp = T // CMP_STRIDE
    src = jnp.stack([k_cmp, v_cmp]).reshape(2, Bsz, G, n_grp, CMP_STRIDE * HD).astype(BF16)
    kvc = _compress(src,
                    jnp.stack([pe_k, pe_v]).reshape(2, 1, CMP_LEN * HD),
                    jnp.stack([w1_k, w1_v]).astype(BF16),
                    jnp.stack([w2_k, w2_v]).astype(BF16))
    bf = lambda t: t.astype(BF16)
    o = _nsa(q, gates, kvc, bf(k_sel), bf(v_sel), bf(k_win), bf(v_win), bias_c, bias_w, cover)
    return o.transpose(0, 3, 1, 2, 4).reshape(Bsz, T, G * J * HD)


def _cover_matrix(T, n_pad):
    n = jnp.arange(n_pad)[:, None] * CMP_STRIDE
    m = jnp.arange(LANES)[None, :] * SEL_LEN
    real = (jnp.arange(n_pad)[:, None] < (T - CMP_LEN) // CMP_STRIDE + 1) & (jnp.arange(LANES)[None, :] < T // SEL_LEN)
    return ((n < m + SEL_LEN) & (n + CMP_LEN > m) & real).astype(BF16)


def kernel(x, c, ada_w, ada_b, norm1_w, norm2_w, w_in, hgrn_lb_logits, hgrn_norm_w, nsa_pe_k, nsa_cmp_w1_k, nsa_cmp_w2_k, nsa_pe_v, nsa_cmp_w1_v, nsa_cmp_w2_v, rel_bias, rw_mu, rw_w0, rw_w2, rw_a0, rw_a2, rw_g2, rw_k_k, rw_k_a, rw_r_k, rw_lnx_w, rw_lnx_b, w_branch, w_out, ffn_w1, ffn_w3, ffn_w2, final_norm_w):
    Bsz, T, D = x.shape
    L = ada_w.shape[0]
    n_pad = T // CMP_STRIDE
    c_pad = jnp.pad(c, ((0, 8 - Bsz), (0, 0)))
    ada = _ada(c_pad, ada_w, ada_b)[:, :Bsz]
    bias_c, bias_w = _bias_tables(rel_bias, T, n_pad)
    cover = _cover_matrix(T, n_pad)
    zeros64 = jnp.zeros((C_DECAY_LORA, C_HEADS * C_HD), F32)
    for l in range(L):
        sh1, sc1, gt1, sh2, sc2, gt2 = (ada[l, :, None, i * D:(i + 1) * D] for i in range(6))
        p = _inproj(x, norm1_w[l].reshape(1, D), sh1, sc1, _pack_w_in(w_in[l]))
        y_a = _hgrn(p, hgrn_lb_logits, hgrn_norm_w[l], l)
        y_b = _nsa_layer(p, nsa_pe_k[l], nsa_cmp_w1_k[l], nsa_cmp_w2_k[l],
                         nsa_pe_v[l], nsa_cmp_w1_v[l], nsa_cmp_w2_v[l], bias_c, bias_w, cover)
        y_c = _rwkv(p, rw_mu[l],
                    jnp.concatenate([rw_w2[l], zeros64], axis=0),
                    jnp.concatenate([zeros64, rw_a2[l]], axis=0),
                    rw_g2[l], rw_w0[l], rw_a0[l], rw_k_k[l], rw_k_a[l], rw_r_k[l], rw_lnx_w[l], rw_lnx_b[l])
        x = _merge(y_a, y_b, y_c, p, x, gt1, w_branch[l].astype(BF16), w_out[l].astype(BF16))
        x = _ffn(x, norm2_w[l].reshape(1, D), sh2, sc2, gt2,
                 ffn_w1[l].astype(BF16), ffn_w3[l].astype(BF16), ffn_w2[l].astype(BF16),
                 final_norm_w.reshape(1, D), l == L - 1)
    return x
```

```python
import functools
import math

import jax
import jax.numpy as jnp
from jax import lax
from jax.experimental import pallas as pl
from jax.experimental.pallas import tpu as pltpu

F32 = jnp.float32
BF16 = jnp.bfloat16
I32 = jnp.int32

LANES = 128
VMEM_LIMIT = 56 * 1024 * 1024

D_MODEL = 1024
DEPTH = 4
A_HEADS, A_DK, A_DV = 4, 128, 128
A_CHUNK, A_SUB = 64, 16
B_HEADS, B_GROUPS, B_HD = 8, 2, 64
B_J = B_HEADS // B_GROUPS
CMP_LEN, CMP_STRIDE, CMP_HIDDEN = 32, 16, 256
SEL_LEN, SEL_TOP, N_LOCAL = 64, 16, 2
WINDOW, Q_BLOCK = 512, 128
N_WIN_TILES = WINDOW // Q_BLOCK + 1
SEL_TILES_PER_STEP = 4
REL_BUCKETS, REL_MAX_DIST = 32, 128
C_HEADS, C_HD = 8, 64
C_CHUNK = 64
C_DECAY_LORA, C_AAA_LORA, C_GATE_LORA = 64, 64, 128
LNX_EPS = 64e-5
D_FF = 2816

COL_A = 0
COL_GATE = 2048 // LANES
COL_C = 5120 // LANES
COL_B = 6912 // LANES
P_COLS = 8320


def _cparams(sem):
    return pltpu.CompilerParams(dimension_semantics=sem, vmem_limit_bytes=VMEM_LIMIT)


def _split3(x):
    hi = x.astype(BF16)
    r1 = x - hi.astype(F32)
    mid = r1.astype(BF16)
    lo = (r1 - mid.astype(F32)).astype(BF16)
    return hi, mid, lo


def _mm(a, b):
    return jnp.dot(a.astype(BF16), b.astype(BF16), preferred_element_type=F32)


def _mm_nt(a, b):
    return lax.dot_general(a.astype(BF16), b.astype(BF16), (((1,), (1,)), ((), ())),
                           preferred_element_type=F32)


def _mm_tn(a, b):
    return lax.dot_general(a.astype(BF16), b.astype(BF16), (((0,), (0,)), ((), ())),
                           preferred_element_type=F32)


def _mm_exact_lhs(a_exact, b):
    a = a_exact.astype(BF16)
    hi, mid, lo = _split3(b)
    d = functools.partial(jnp.dot, preferred_element_type=F32)
    return d(a, hi) + d(a, mid) + d(a, lo)


def _mm_exact_rhs(a, b_exact):
    b = b_exact.astype(BF16)
    hi, mid, lo = _split3(a)
    d = functools.partial(jnp.dot, preferred_element_type=F32)
    return d(hi, b) + d(mid, b) + d(lo, b)


def _mm_x3(a, b):
    ah = a.astype(BF16)
    al = (a - ah.astype(F32)).astype(BF16)
    bh = b.astype(BF16)
    bl = (b - bh.astype(F32)).astype(BF16)
    d = functools.partial(jnp.dot, preferred_element_type=F32)
    return d(ah, bh) + d(ah, bl) + d(al, bh)


def _sigmoid(x):
    return 1.0 / (1.0 + jnp.exp(-x))


def _silu(x):
    return x * _sigmoid(x)


def _log_sigmoid(x):
    return jnp.minimum(x, 0.0) - jnp.log(1.0 + jnp.exp(-jnp.abs(x)))


def _iota(shape, dim):
    return lax.broadcasted_iota(I32, shape, dim)


def _ada_kernel(c_ref, w_ref, b_ref, o_ref):
    c = c_ref[...]
    o_ref[0] = _mm_x3(_silu(c), w_ref[0]) + b_ref[0]


def _ada(c_pad, ada_w, ada_b):
    L = ada_w.shape[0]
    n_out = ada_w.shape[2]
    tn = D_MODEL
    return pl.pallas_call(
        _ada_kernel,
        grid=(L, n_out // tn),
        in_specs=[pl.BlockSpec((8, D_MODEL), lambda l, j: (0, 0)),
                  pl.BlockSpec((1, D_MODEL, tn), lambda l, j: (l, 0, j)),
                  pl.BlockSpec((1, 1, tn), lambda l, j: (l, 0, j))],
        out_specs=pl.BlockSpec((1, 8, tn), lambda l, j: (l, 0, j)),
        out_shape=jax.ShapeDtypeStruct((L, 8, n_out), F32),
        compiler_params=_cparams(("arbitrary", "arbitrary")),
        name="ada_ln",
    )(c_pad, ada_w, ada_b.reshape(L, 1, n_out))


def _norm_mod(x, nw, sh, sc):
    y = x * lax.rsqrt(jnp.mean(x * x, axis=-1, keepdims=True) + 1e-6) * nw
    return y * (1.0 + sc) + sh


def _inproj_kernel(x_ref, nw_ref, sh_ref, sc_ref, w_ref, o_ref, h_scr):
    @pl.when(pl.program_id(2) == 0)
    def _():
        h_scr[...] = _norm_mod(x_ref[0], nw_ref[...], sh_ref[0], sc_ref[0]).astype(BF16)

    o_ref[0] = jnp.dot(h_scr[...], w_ref[...], preferred_element_type=F32)


def _inproj(x, nw, sh, sc, w):
    Bsz, T, D = x.shape
    n_out = w.shape[1]
    tm, tn = 512, 640
    return pl.pallas_call(
        _inproj_kernel,
        grid=(Bsz, T // tm, n_out // tn),
        in_specs=[pl.BlockSpec((1, tm, D), lambda b, i, n: (b, i, 0)),
                  pl.BlockSpec((1, D), lambda b, i, n: (0, 0)),
                  pl.BlockSpec((1, 1, D), lambda b, i, n: (b, 0, 0)),
                  pl.BlockSpec((1, 1, D), lambda b, i, n: (b, 0, 0)),
                  pl.BlockSpec((D, tn), lambda b, i, n: (0, n))],
        out_specs=pl.BlockSpec((1, tm, tn), lambda b, i, n: (b, i, n)),
        out_shape=jax.ShapeDtypeStruct((Bsz, T, n_out), F32),
        scratch_shapes=[pltpu.VMEM((tm, D), BF16)],
        compiler_params=_cparams(("arbitrary", "arbitrary", "arbitrary")),
        name="in_proj",
    )(x, nw, sh, sc, w)


def _hgrn_kernel(layer, lbl_ref, nw_ref, q_ref, f_ref, i_ref, g_ref, o_ref, s_scr):
    C, c = A_CHUNK, A_SUB
    slabs = [slice(h * A_DK, (h + 1) * A_DK) for h in range(A_HEADS)]

    @pl.when(pl.program_id(1) == 0)
    def _():
        s_scr[...] = jnp.zeros_like(s_scr)

    logits = lbl_ref[...]
    ex = jnp.exp(logits - jnp.max(logits, axis=0, keepdims=True))
    sm = ex / jnp.sum(ex, axis=0, keepdims=True)
    lb = jnp.zeros((1, logits.shape[1]), F32)
    for i in range(1, layer + 1):
        lb = lb + sm[i:i + 1, :]

    q = _silu(q_ref[0])
    fr = f_ref[0]
    v = i_ref[0]
    la = jnp.log(lb)
    lc = jnp.log(1.0 - lb) + _log_sigmoid(fr)
    log_f = jnp.maximum(la, lc) + jnp.log(1.0 + jnp.exp(-jnp.abs(la - lc)))
    k = (1.0 - lb) * _sigmoid(-fr)

    tri = (_iota((C, C), 0) >= _iota((C, C), 1)).astype(BF16)
    b = _mm_exact_lhs(tri, log_f)
    b_last = b[C - 1:C, :]

    qe = q * jnp.exp(b)
    o = [_mm_nt(qe[:, sl], s_scr[h]) for h, sl in enumerate(slabs)]
    off = [[jnp.zeros((c, A_DV), F32)] for _ in slabs]
    for blk in range(1, C // c):
        lo = blk * c
        ref = b[lo - 1:lo, :]
        qi = q[lo:lo + c] * jnp.exp(b[lo:lo + c] - ref)
        ki = k[:lo] * jnp.exp(ref - b[:lo])
        att = [_mm_nt(qi[:, sl], ki[:, sl]) for sl in slabs]
        for h, sl in enumerate(slabs):
            off[h].append(_mm(att[h], v[:lo, sl]))
    o = [o[h] + jnp.concatenate(off[h], axis=0) for h in range(len(slabs))]
    row = _iota((C, 1), 0) % c
    for j in range(c):
        if j == 0:
            kj, bj, vj = k, b, v
        else:
            kj, bj, vj = (pltpu.roll(t, j, 0) for t in (k, b, v))
        ok = row >= j
        term = jnp.where(ok, q * kj * jnp.exp(jnp.where(ok, b - bj, 0.0)), 0.0)
        for h, sl in enumerate(slabs):
            o[h] = o[h] + jnp.sum(term[:, sl], axis=1, keepdims=True) * vj[:, sl]

    kd = k * jnp.exp(b_last - b)
    p_end = jnp.exp(b_last)
    for h, sl in enumerate(slabs):
        s_scr[h] = s_scr[h] * p_end[:, sl] + _mm_tn(v[:, sl], kd[:, sl])

    gate = _sigmoid(g_ref[0])
    for h, sl in enumerate(slabs):
        oh = o[h]
        oh = oh * lax.rsqrt(jnp.mean(oh * oh, axis=-1, keepdims=True) + 1e-5) * nw_ref[:, sl]
        o_ref[0, :, sl] = oh * gate[:, sl]


def _hgrn(p, lb_logits, norm_w, layer):
    Bsz, T, _ = p.shape
    C = A_CHUNK
    L = lb_logits.shape[0]
    W = A_HEADS * A_DK
    cw = COL_A * LANES // W

    def col(i):
        return pl.BlockSpec((1, C, W), lambda b, c: (b, c, cw + i))

    return pl.pallas_call(
        functools.partial(_hgrn_kernel, layer),
        grid=(Bsz, T // C),
        in_specs=[pl.BlockSpec((L, W), lambda b, c: (0, 0)),
                  pl.BlockSpec((1, A_HEADS * A_DV), lambda b, c: (0, 0)),
                  col(0), col(1), col(2), col(3)],
        out_specs=pl.BlockSpec((1, C, A_HEADS * A_DV), lambda b, c: (b, c, 0)),
        out_shape=jax.ShapeDtypeStruct((Bsz, T, A_HEADS * A_DV), F32),
        scratch_shapes=[pltpu.VMEM((A_HEADS, A_DV, A_DK), F32)],
        compiler_params=_cparams(("arbitrary", "arbitrary")),
        name="hgrn2",
    )(lb_logits, norm_w.reshape(1, -1), p, p, p, p)


def _rwkv_kernel(r_ref, k_ref, v_ref, wa_ref, xg_ref, mu_ref,
                 w2_ref, a2_ref, g2_ref, w0_ref, a0_ref, kk_ref, ka_ref, rk_ref, lnw_ref, lnb_ref,
                 o_ref, s_scr, pr_scr, pk_scr, pv_scr, pwa_scr, pg_scr):
    C = C_CHUNK
    W = C_HEADS * C_HD
    n_hp = W // LANES

    @pl.when(pl.program_id(1) == 0)
    def _():
        s_scr[...] = jnp.zeros_like(s_scr)
        for scr in (pr_scr, pk_scr, pv_scr, pwa_scr, pg_scr):
            scr[...] = jnp.zeros_like(scr)

    row0 = _iota((C, 1), 0) == 0

    def shifted(ref, prev_scr, mu):
        cur = ref[0]
        prev = jnp.where(row0, prev_scr[...], pltpu.roll(cur, 1, 0))
        prev_scr[...] = cur[C - 1:C, :]
        return cur + (prev - cur) * mu

    r = shifted(r_ref, pr_scr, mu_ref[:, 0:W])
    k = shifted(k_ref, pk_scr, mu_ref[:, W:2 * W])
    v = shifted(v_ref, pv_scr, mu_ref[:, 2 * W:3 * W])
    xwa = shifted(wa_ref, pwa_scr, mu_ref[:, 3 * W:3 * W + LANES])
    xg = shifted(xg_ref, pg_scr, mu_ref[:, 3 * W + LANES:])

    w = w0_ref[...] + _mm_x3(jnp.tanh(xwa), w2_ref[...])
    sp = jnp.maximum(-w, 0.0) + jnp.log(1.0 + jnp.exp(-jnp.abs(w)))
    lw = -jnp.exp(-sp - 0.5)
    a = _sigmoid(a0_ref[...] + _mm_x3(xwa, a2_ref[...]))
    g = _mm_x3(_sigmoid(xg), g2_ref[...])

    tri_incl = (_iota((C, C), 0) >= _iota((C, C), 1)).astype(BF16)
    b = _mm_exact_lhs(tri_incl, lw)
    b_last = b[C - 1:C, :]
    p_end = jnp.exp(b_last)
    inv_p = jnp.exp(-b)
    e_end = jnp.exp(b_last - b)
    kk_raw = k * kk_ref[...]
    k = k * (1.0 + (a - 1.0) * ka_ref[...])
    bonus_in = r * k * rk_ref[...]
    r_t = r * jnp.exp(b)
    p_prev = jnp.exp(b - lw)

    head0 = _iota((1, LANES), 1) < C_HD
    bd_f = ((_iota((LANES, LANES), 0) < C_HD) == (_iota((LANES, LANES), 1) < C_HD)).astype(F32)
    ti = _iota((2 * C, 2 * C), 0) % C
    tj = _iota((2 * C, 2 * C), 1) % C
    lower_strict = ti > tj
    lower_incl = ti >= tj

    slabs = [slice(hp * LANES, (hp + 1) * LANES) for hp in range(n_hp)]
    dot = functools.partial(jnp.dot, preferred_element_type=F32)
    bd_b = bd_f.astype(BF16)

    def stack(x):
        return jnp.concatenate([jnp.where(head0, x, 0.0), jnp.where(head0, 0.0, x)], axis=0)

    def group_sums(x):
        st = jnp.concatenate([x[:, sl] for sl in slabs], axis=0)
        hi = st.astype(BF16)
        lo = (st - hi.astype(F32)).astype(BF16)
        out = dot(hi, bd_b) + dot(lo, bd_b)
        return jnp.concatenate([out[i * C:(i + 1) * C] for i in range(n_hp)], axis=1)

    kk = kk_raw * lax.rsqrt(jnp.maximum(group_sums(kk_raw * kk_raw), 1e-24))
    al = a * kk
    bonus = group_sums(bonus_in) * v
    kap = kk * p_prev
    al_h = al * inv_p
    k_h = k * inv_p
    al_e = al * e_end
    k_e = k * e_end

    C2 = 2 * C
    lhs = [jnp.concatenate([stack(kap[:, sl]), stack(r_t[:, sl])], axis=0) for sl in slabs]
    v_st = [stack(v[:, sl]) for sl in slabs]
    big = [_mm_nt(lhs[hp], jnp.concatenate([stack(al_h[:, sl]), stack(k_h[:, sl])], axis=0))
           for hp, sl in enumerate(slabs)]
    a_ak = [jnp.where(lower_strict, m[:C2, :C2], 0.0) for m in big]
    b_al = [jnp.where(lower_incl, m[C2:, :C2], 0.0) for m in big]
    kv = [_mm(jnp.concatenate([jnp.where(lower_strict, m[:C2, C2:], 0.0),
                               jnp.where(lower_incl, m[C2:, C2:], 0.0)], axis=0), v_st[hp])
          for hp, m in enumerate(big)]

    eye = (_iota((C2, C2), 0) == _iota((C2, C2), 1)).astype(F32)
    n = [-m for m in a_ak]
    x = [eye + m for m in n]
    n = [_mm_x3(m, m) for m in n]
    power = 2
    while 2 * power < C:
        for hp in range(n_hp):
            res = _mm_x3(jnp.concatenate([x[hp], n[hp]], axis=0), n[hp])
            x[hp] = x[hp] + res[:C2]
            n[hp] = res[C2:]
        power *= 2
    x = [x[hp] + _mm_x3(x[hp], n[hp]) for hp in range(n_hp)]

    y_parts = []
    for hp, sl in enumerate(slabs):
        s = s_scr[hp]
        ks = _mm_nt(lhs[hp], s)
        u_st = -_mm_x3(x[hp], ks[:C2] + kv[hp][:C2])
        y_st = ks[C2:] + _mm(b_al[hp], u_st) + kv[hp][C2:]
        y_parts.append(y_st[:C] + y_st[C:])
        s_scr[hp] = s * p_end[:, sl] + _mm_tn(jnp.concatenate([u_st, v_st[hp]], axis=0),
                                              jnp.concatenate([stack(al_e[:, sl]), stack(k_e[:, sl])], axis=0))

    y = jnp.concatenate(y_parts, axis=1)
    inv_hd = 1.0 / C_HD
    yc = y - group_sums(y) * inv_hd
    var = group_sums(yc * yc) * inv_hd
    y = yc * lax.rsqrt(var + LNX_EPS) * lnw_ref[...] + lnb_ref[...]
    o_ref[0] = (y + bonus) * g


def _rwkv(p, mu, w2p, a2p, g2, w0, a0, k_k, k_a, r_k, lnx_w, lnx_b):
    Bsz, T, _ = p.shape
    C = C_CHUNK
    W = C_HEADS * C_HD
    n_hp = W // LANES
    cw = COL_C * LANES // W

    def wide(i):
        return pl.BlockSpec((1, C, W), lambda b, c: (b, c, cw + i))

    def narrow(i):
        return pl.BlockSpec((1, C, LANES), lambda b, c: (b, c, COL_C + 3 * n_hp + i))

    def const(shape):
        return pl.BlockSpec(shape, lambda b, c: (0,) * len(shape))

    row = lambda t: t.reshape(1, -1)
    vecs = [row(t) for t in (w0, a0, k_k, k_a, r_k, lnx_w, lnx_b)]
    return pl.pallas_call(
        _rwkv_kernel,
        grid=(Bsz, T // C),
        in_specs=[wide(0), wide(1), wide(2), narrow(0), narrow(1), const((1, mu.shape[0])),
                  const(w2p.shape), const(a2p.shape), const(g2.shape)] + [const((1, W))] * 7,
        out_specs=pl.BlockSpec((1, C, W), lambda b, c: (b, c, 0)),
        out_shape=jax.ShapeDtypeStruct((Bsz, T, W), F32),
        scratch_shapes=[pltpu.VMEM((n_hp, LANES, LANES), F32)] + [pltpu.VMEM((1, W), F32)] * 3
                       + [pltpu.VMEM((1, LANES), F32)] * 2,
        compiler_params=_cparams(("arbitrary", "arbitrary")),
        name="rwkv7",
    )(p, p, p, p, p, row(mu), w2p, a2p, g2, *vecs)


def _t5_thresholds():
    max_exact = REL_BUCKETS // 2
    out = []
    for kk in range(1, REL_BUCKETS - max_exact):
        n = max_exact
        while int(math.log(n / max_exact) / math.log(REL_MAX_DIST / max_exact) * (REL_BUCKETS - max_exact)) < kk:
            n += 1
        out.append(n)
    return out


def _bias_from_dist(dist, rb_ref, head):
    n = jnp.maximum(dist, 0)
    max_exact = REL_BUCKETS // 2
    large = jnp.full(n.shape, max_exact, I32)
    for thr in _t5_thresholds():
        large = large + (n >= thr).astype(I32)
    bucket = jnp.where(n < max_exact, n, large)
    out = jnp.zeros(n.shape, F32)
    for j in range(REL_BUCKETS):
        out = jnp.where(bucket == j, rb_ref[j, head], out)
    return out


def _bias_cmp_kernel(rb_ref, o_ref):
    g, qb = pl.program_id(0), pl.program_id(1)
    n_pad = o_ref.shape[1]
    t = qb * Q_BLOCK + _iota((n_pad, Q_BLOCK), 1)
    cmp_end = _iota((n_pad, Q_BLOCK), 0) * CMP_STRIDE + CMP_LEN - 1
    for j in range(B_J):
        o_ref[0, :, j * Q_BLOCK:(j + 1) * Q_BLOCK] = _bias_from_dist(t - cmp_end, rb_ref, g * B_J + j)


def _bias_win_kernel(rb_ref, o_ref):
    g, d = pl.program_id(0), pl.program_id(1)
    dist = d * Q_BLOCK + _iota((Q_BLOCK, Q_BLOCK), 1) - _iota((Q_BLOCK, Q_BLOCK), 0)
    for j in range(B_J):
        o_ref[0, 0, :, j * Q_BLOCK:(j + 1) * Q_BLOCK] = _bias_from_dist(dist, rb_ref, g * B_J + j)


def _bias_tables(rel_bias, T, n_pad):
    smem = pl.BlockSpec(memory_space=pltpu.SMEM)
    jq = B_J * Q_BLOCK
    n_q = T // Q_BLOCK
    bias_c = pl.pallas_call(
        _bias_cmp_kernel,
        grid=(B_GROUPS, n_q),
        in_specs=[smem],
        out_specs=pl.BlockSpec((1, n_pad, jq), lambda g, i: (g, 0, i)),
        out_shape=jax.ShapeDtypeStruct((B_GROUPS, n_pad, n_q * jq), F32),
        compiler_params=_cparams(("arbitrary", "arbitrary")),
        name="nsa_bias_cmp",
    )(rel_bias)
    bias_w = pl.pallas_call(
        _bias_win_kernel,
        grid=(B_GROUPS, N_WIN_TILES),
        in_specs=[smem],
        out_specs=pl.BlockSpec((1, 1, Q_BLOCK, jq), lambda g, d: (g, d, 0, 0)),
        out_shape=jax.ShapeDtypeStruct((B_GROUPS, N_WIN_TILES, Q_BLOCK, jq), F32),
        compiler_params=_cparams(("arbitrary", "arbitrary")),
        name="nsa_bias_win",
    )(rel_bias)
    return bias_c, bias_w


def _compress_kernel(x_ref, pe_ref, w1_ref, w2_ref, o_ref):
    x = x_ref[0, 0, 0]
    half = CMP_STRIDE * B_HD
    w1a = w1_ref[0, :half, :]
    w1b = w1_ref[0, half:, :]
    pe = pe_ref[0]
    n_grp = x.shape[0]
    hb = _mm(x, w1b)
    h = _mm(x, w1a) + pltpu.roll(hb, n_grp - 1, 0) + _mm(pe[:, :half], w1a) + _mm(pe[:, half:], w1b)
    out = _mm(_silu(h), w2_ref[0])
    o_ref[0, 0, 0] = jnp.where(_iota(out.shape, 0) < n_grp - 1, out, 0.0)


def _compress(src, pe, w1, w2):
    _, Bsz, G, n_grp, width = src.shape
    return pl.pallas_call(
        _compress_kernel,
        grid=(2, Bsz, G),
        in_specs=[pl.BlockSpec((1, 1, 1, n_grp, width), lambda s, b, g: (s, b, g, 0, 0)),
                  pl.BlockSpec((1, 1, CMP_LEN * B_HD), lambda s, b, g: (s, 0, 0)),
                  pl.BlockSpec((1, CMP_LEN * B_HD, CMP_HIDDEN), lambda s, b, g: (s, 0, 0)),
                  pl.BlockSpec((1, CMP_HIDDEN, B_HD), lambda s, b, g: (s, 0, 0))],
        out_specs=pl.BlockSpec((1, 1, 1, n_grp, B_HD), lambda s, b, g: (s, b, g, 0, 0)),
        out_shape=jax.ShapeDtypeStruct((2, Bsz, G, n_grp, B_HD), F32),
        compiler_params=_cparams(("arbitrary", "arbitrary", "arbitrary")),
        name="nsa_compress",
    )(src, pe, w1, w2)


def _nsa_kernel(q_ref, gate_ref, kc_ref, vc_ref, ks_ref, vs_ref, kw_ref, vw_ref, bc_ref, bw_ref, cov_ref,
                o_ref):
    QB, J, HD = Q_BLOCK, B_J, B_HD
    JQ = J * QB
    qb = pl.program_id(2)
    t0 = qb * QB
    n_pad = kc_ref.shape[3]
    NEG = -jnp.inf
    dot = functools.partial(jnp.dot, preferred_element_type=F32)

    q_t = (q_ref[0, 0, 0] * (HD ** -0.5)).astype(BF16)
    i_lane = _iota((1, JQ), 1) % QB
    t_lane = t0 + i_lane
    t_q = t0 + _iota((1, QB), 1)

    def finish(e_sum):
        return 1.0 / jnp.maximum(e_sum, 1e-30)

    def softmax_parts(scores):
        m = functools.reduce(jnp.maximum, [jnp.max(sc, axis=0, keepdims=True) for sc in scores])
        m = jnp.where(m == NEG, 0.0, m)
        es = [jnp.exp(sc - m) for sc in scores]
        tot = functools.reduce(jnp.add, [jnp.sum(e, axis=0, keepdims=True) for e in es])
        return es, finish(tot)

    n_col = _iota((n_pad, 1), 0)
    mask_c = ((n_col * CMP_STRIDE + CMP_LEN - 1) <= t_lane) & (n_col < n_pad - 1)
    s = jnp.where(mask_c, dot(kc_ref[0, 0, 0].astype(BF16), q_t) + bc_ref[0], NEG)
    (e_c,), inv_c = softmax_parts([s])
    p_c = e_c * inv_c
    o_c = _mm_tn(vc_ref[0, 0, 0], p_c)

    p_sum = p_c[:, 0:QB]
    for j in range(1, J):
        p_sum = p_sum + p_c[:, j * QB:(j + 1) * QB]
    ph = p_sum.astype(BF16)
    pl_ = (p_sum - ph.astype(F32)).astype(BF16)
    cov = cov_ref[...]
    imp = dot(cov, ph) + dot(cov, pl_)
    n_blk = cov.shape[0]
    m_col = _iota((n_blk, 1), 0)
    cur = t_q // SEL_LEN
    causal = m_col <= cur
    forced = (m_col == 0) | ((m_col > cur - N_LOCAL) & causal)
    score = jnp.where(causal, jnp.where(forced, jnp.inf, imp), NEG)
    n_real = ks_ref.shape[2] // SEL_LEN
    st = score[:n_real, :]
    blk = _iota((n_real, 1), 0)
    cnt = jnp.zeros((n_real, QB), F32)
    for mp in range(n_real):
        other = st[mp:mp + 1, :]
        beats = (other > st) | ((other == st) & (mp < blk))
        cnt = cnt + beats.astype(F32)
    sel_t = (cnt < float(SEL_TOP)).astype(BF16)
    if n_real < n_blk:
        sel_t = jnp.concatenate([sel_t, jnp.zeros((n_blk - n_real, QB), BF16)], axis=0)

    key_col = _iota((QB, 1), 0)
    blk_lane = _iota((1, n_blk), 1)
    blocks_per_tile = QB // SEL_LEN

    n_tiles = ks_ref.shape[2] // QB

    def sel_scores(kt):
        ktc = jnp.minimum(kt, n_tiles - 1)
        start = pl.multiple_of(ktc * QB, QB)
        k_t = ks_ref[0, 0, pl.ds(start, QB), :]
        expand = ((key_col // SEL_LEN + kt * blocks_per_tile) == blk_lane).astype(BF16)
        member = dot(expand, sel_t) > 0.5
        ok = member & ((kt * QB + key_col) <= t_q)
        ok = jnp.concatenate([ok] * J, axis=1)
        sc = dot(k_t, q_t) + bw_ref[0, jnp.clip(qb - kt, 0, 2)]
        return jnp.where(ok, sc, NEG), vs_ref[0, 0, ktc]

    def sel_step(it, carry):
        m_old, l_old, acc = carry
        tiles = [sel_scores(it * SEL_TILES_PER_STEP + u) for u in range(SEL_TILES_PER_STEP)]
        m_new = functools.reduce(jnp.maximum, [m_old] + [jnp.max(sc, axis=0, keepdims=True) for sc, _ in tiles])
        m_use = jnp.where(m_new == NEG, 0.0, m_new)
        alpha = jnp.exp(m_old - m_use)
        l_new = alpha * l_old
        acc = alpha * acc
        for sc, v_t in tiles:
            pr = jnp.exp(sc - m_use)
            l_new = l_new + jnp.sum(pr, axis=0, keepdims=True)
            acc = acc + dot(v_t, pr.astype(BF16))
        return m_new, l_new, acc

    init = (jnp.full((1, JQ), NEG, F32), jnp.zeros((1, JQ), F32), jnp.zeros((HD, JQ), F32))
    n_steps = (qb + SEL_TILES_PER_STEP) // SEL_TILES_PER_STEP
    _, l_s, acc_s = lax.fori_loop(0, n_steps, sel_step, init)
    o_s = acc_s * finish(l_s)

    scores, values = [], []
    for d in range(N_WIN_TILES):
        kt = jnp.maximum(qb - d, 0)
        start = pl.multiple_of(kt * QB, QB)
        sc = dot(kw_ref[0, 0, pl.ds(start, QB), :], q_t) + bw_ref[0, d]
        ok = qb >= d
        if d == 0:
            ok = ok & (key_col <= i_lane)
        elif d == N_WIN_TILES - 1:
            ok = ok & (key_col > i_lane)
        else:
            ok = jnp.broadcast_to(ok, (QB, JQ))
        scores.append(jnp.where(ok, sc, NEG))
        values.append(vw_ref[0, 0, kt])
    es, inv_w = softmax_parts(scores)
    acc_w = functools.reduce(jnp.add, [dot(v_t, e.astype(BF16)) for v_t, e in zip(values, es)])
    o_w = acc_w * inv_w

    g = _sigmoid(gate_ref[0, 0, 0])
    o_ref[0, 0, 0] = g[0:1] * o_c + g[1:2] * o_s + g[2:3] * o_w


def _nsa(q_t, gates, kvc, ks, vs_t, kw, vw_t, bias_c, bias_w, cover_t):
    Bsz, G, n_q, HD, JQ = q_t.shape
    T = ks.shape[2]
    n_pad = kvc.shape[3]
    QB = Q_BLOCK
    qspec = pl.BlockSpec((1, 1, 1, HD, JQ), lambda b, g, i: (b, g, i, 0, 0))
    kspec = pl.BlockSpec((1, 1, T, HD), lambda b, g, i: (b, g, 0, 0))
    vspec = pl.BlockSpec((1, 1, n_q, HD, QB), lambda b, g, i: (b, g, 0, 0, 0))
    return pl.pallas_call(
        _nsa_kernel,
        grid=(Bsz, G, n_q),
        in_specs=[qspec,
                  pl.BlockSpec((1, 1, 1, 3, JQ), lambda b, g, i: (b, g, i, 0, 0)),
                  pl.BlockSpec((1, 1, 1, n_pad, HD), lambda b, g, i: (0, b, g, 0, 0)),
                  pl.BlockSpec((1, 1, 1, n_pad, HD), lambda b, g, i: (1, b, g, 0, 0)),
                  kspec, vspec, kspec, vspec,
                  pl.BlockSpec((1, n_pad, JQ), lambda b, g, i: (g, 0, i)),
                  pl.BlockSpec((1, N_WIN_TILES, QB, JQ), lambda b, g, i: (g, 0, 0, 0)),
                  pl.BlockSpec(cover_t.shape, lambda b, g, i: (0, 0))],
        out_specs=qspec,
        out_shape=jax.ShapeDtypeStruct(q_t.shape, F32),
        compiler_params=_cparams(("arbitrary", "arbitrary", "arbitrary")),
        name="nsa_attention",
    )(q_t, gates, kvc, kvc, ks, vs_t, kw, vw_t, bias_c, bias_w, cover_t)


def _merge_kernel(ya_ref, yb_ref, yc_ref, ga_ref, gb_ref, gc_ref, x_ref, gt_ref, wb_ref, wo_ref, o_ref):
    merged = (_sigmoid(ga_ref[0]) * _mm(ya_ref[0], wb_ref[0])
              + _sigmoid(gb_ref[0]) * _mm(yb_ref[0], wb_ref[1])
              + _sigmoid(gc_ref[0]) * _mm(yc_ref[0], wb_ref[2]))
    o_ref[0] = x_ref[0] + gt_ref[0] * _mm(merged, wo_ref[...])


def _merge(ya, yb, yc, p, x, gt, wb, wo):
    Bsz, T, D = x.shape
    tm = 512
    bw = ya.shape[2]
    gate_blk = COL_GATE * LANES // D
    yspec = pl.BlockSpec((1, tm, bw), lambda b, i: (b, i, 0))

    def gspec(k):
        return pl.BlockSpec((1, tm, D), lambda b, i: (b, i, gate_blk + k))

    return pl.pallas_call(
        _merge_kernel,
        grid=(Bsz, T // tm),
        in_specs=[yspec, yspec, yspec, gspec(0), gspec(1), gspec(2),
                  pl.BlockSpec((1, tm, D), lambda b, i: (b, i, 0)),
                  pl.BlockSpec((1, 1, D), lambda b, i: (b, 0, 0)),
                  pl.BlockSpec(wb.shape, lambda b, i: (0, 0, 0)),
                  pl.BlockSpec(wo.shape, lambda b, i: (0, 0))],
        out_specs=pl.BlockSpec((1, tm, D), lambda b, i: (b, i, 0)),
        out_shape=jax.ShapeDtypeStruct(x.shape, F32),
        compiler_params=_cparams(("arbitrary", "arbitrary")),
        name="merge_out_proj",
    )(ya, yb, yc, p, p, p, x, gt, wb, wo)


def _ffn_kernel(final, x_ref, nw_ref, sh_ref, sc_ref, gt_ref, w1_ref, w3_ref, w2_ref, fw_ref, o_ref,
                h_scr, acc_scr):
    f = pl.program_id(2)

    @pl.when(f == 0)
    def _():
        h_scr[...] = _norm_mod(x_ref[0], nw_ref[...], sh_ref[0], sc_ref[0]).astype(BF16)
        acc_scr[...] = jnp.zeros_like(acc_scr)

    h = h_scr[...]
    u = _silu(jnp.dot(h, w1_ref[...], preferred_element_type=F32)) * jnp.dot(h, w3_ref[...],
                                                                           preferred_element_type=F32)
    acc_scr[...] += _mm(u, w2_ref[...])

    @pl.when(f == pl.num_programs(2) - 1)
    def _():
        x = x_ref[0] + gt_ref[0] * acc_scr[...]
        if final:
            x = x * lax.rsqrt(jnp.mean(x * x, axis=-1, keepdims=True) + 1e-6) * fw_ref[...]
        o_ref[0] = x


def _ffn(x, nw, sh, sc, gt, w1, w3, w2, fw, final):
    Bsz, T, D = x.shape
    tm, tf = 512, 256
    n_ff = w1.shape[1]
    vec = pl.BlockSpec((1, D), lambda b, i, f: (0, 0))
    bvec = pl.BlockSpec((1, 1, D), lambda b, i, f: (b, 0, 0))
    xspec = pl.BlockSpec((1, tm, D), lambda b, i, f: (b, i, 0))
    return pl.pallas_call(
        functools.partial(_ffn_kernel, final),
        grid=(Bsz, T // tm, n_ff // tf),
        in_specs=[xspec, vec, bvec, bvec, bvec,
                  pl.BlockSpec((D, tf), lambda b, i, f: (0, f)),
                  pl.BlockSpec((D, tf), lambda b, i, f: (0, f)),
                  pl.BlockSpec((tf, D), lambda b, i, f: (f, 0)),
                  vec],
        out_specs=xspec,
        out_shape=jax.ShapeDtypeStruct(x.shape, F32),
        scratch_shapes=[pltpu.VMEM((tm, D), BF16), pltpu.VMEM((tm, D), F32)],
        compiler_params=_cparams(("arbitrary", "arbitrary", "arbitrary")),
        name="ffn_swiglu",
    )(x, nw, sh, sc, gt, w1, w3, w2, fw)


def _pack_w_in(w):
    a_in, b_in, c_in = 2048, 1304, 1792
    wa = w[:, :a_in]
    wb = w[:, a_in:a_in + b_in]
    wc = w[:, a_in + b_in:a_in + b_in + c_in]
    wg = w[:, a_in + b_in + c_in:]
    wb = jnp.pad(wb, ((0, 0), (0, P_COLS - COL_B * LANES - b_in)))
    return jnp.concatenate([wa, wg, wc, wb], axis=1).astype(BF16)


def _nsa_layer(p, pe_k, w1_k, w2_k, pe_v, w1_v, w2_v, bias_c, bias_w, cover):
    Bsz, T, _ = p.shape
    G, J, HD = B_GROUPS, B_J, B_HD
    base = COL_B * LANES
    pb = p[:, :, base:]
    QB = Q_BLOCK
    n_q = T // QB
    q_t = (pb[:, :, :G * J * HD].reshape(Bsz, n_q, QB, G, J, HD).transpose(0, 3, 1, 5, 4, 2)
           .reshape(Bsz, G, n_q, HD, J * QB))
    off = G * J * HD

    def kv(i):
        return pb[:, :, off + i * G * HD: off + (i + 1) * G * HD].reshape(Bsz, T, G, HD)

    k_cmp, v_cmp, k_sel, v_sel, k_win, v_win = (kv(i) for i in range(6))
    rows = lambda t: t.transpose(0, 2, 1, 3)
    tiles_t = lambda t: t.reshape(Bsz, n_q, QB, G, HD).transpose(0, 3, 1, 4, 2).astype(BF16)
    goff = off + 6 * G * HD
    gates = (pb[:, :, goff:goff + 3 * G * J].reshape(Bsz, n_q, QB, G, J, 3).transpose(0, 3, 1, 5, 4, 2)
             .reshape(Bsz, G, n_q, 3, J * QB))
    n_grp = T // CMP_STRIDE
    src = jnp.stack([rows(k_cmp), rows(v_cmp)]).reshape(2, Bsz, G, n_grp, CMP_STRIDE * HD).astype(BF16)
    kvc = _compress(src,
                    jnp.stack([pe_k, pe_v]).reshape(2, 1, CMP_LEN * HD),
                    jnp.stack([w1_k, w1_v]).astype(BF16),
                    jnp.stack([w2_k, w2_v]).astype(BF16))
    o_t = _nsa(q_t, gates, kvc, rows(k_sel).astype(BF16), tiles_t(v_sel), rows(k_win).astype(BF16), tiles_t(v_win),
               bias_c, bias_w, cover)
    return (o_t.reshape(Bsz, G, n_q, HD, J, QB).transpose(0, 2, 5, 1, 4, 3).reshape(Bsz, T, G * J * HD))


def _cover_matrix(T, n_pad):
    n = jnp.arange(n_pad)[None, :] * CMP_STRIDE
    m = jnp.arange(LANES)[:, None] * SEL_LEN
    real = (jnp.arange(n_pad)[None, :] < (T - CMP_LEN) // CMP_STRIDE + 1) & (jnp.arange(LANES)[:, None] < T // SEL_LEN)
    return ((n < m + SEL_LEN) & (n + CMP_LEN > m) & real).astype(BF16)


def kernel(x, c, ada_w, ada_b, norm1_w, norm2_w, w_in, hgrn_lb_logits, hgrn_norm_w, nsa_pe_k, nsa_cmp_w1_k, nsa_cmp_w2_k, nsa_pe_v, nsa_cmp_w1_v, nsa_cmp_w2_v, rel_bias, rw_mu, rw_w0, rw_w2, rw_a0, rw_a2, rw_g2, rw_k_k, rw_k_a, rw_r_k, rw_lnx_w, rw_lnx_b, w_branch, w_out, ffn_w1, ffn_w3, ffn_w2, final_norm_w):
    Bsz, T, D = x.shape
    L = ada_w.shape[0]
    n_pad = T // CMP_STRIDE
    c_pad = jnp.pad(c, ((0, 8 - Bsz), (0, 0)))
    ada = _ada(c_pad, ada_w, ada_b)[:, :Bsz]
    bias_c, bias_w = _bias_tables(rel_bias, T, n_pad)
    cover = _cover_matrix(T, n_pad)
    zeros64 = jnp.zeros((C_DECAY_LORA, C_HEADS * C_HD), F32)
    for l in range(L):
        sh1, sc1, gt1, sh2, sc2, gt2 = (ada[l, :, None, i * D:(i + 1) * D] for i in range(6))
        p = _inproj(x, norm1_w[l].reshape(1, D), sh1, sc1, _pack_w_in(w_in[l]))
        y_a = _hgrn(p, hgrn_lb_logits, hgrn_norm_w[l], l)
        y_b = _nsa_layer(p, nsa_pe_k[l], nsa_cmp_w1_k[l], nsa_cmp_w2_k[l],
                         nsa_pe_v[l], nsa_cmp_w1_v[l], nsa_cmp_w2_v[l], bias_c, bias_w, cover)
        y_c = _rwkv(p, rw_mu[l],
                    jnp.concatenate([rw_w2[l], zeros64], axis=0),
                    jnp.concatenate([zeros64, rw_a2[l]], axis=0),
                    rw_g2[l], rw_w0[l], rw_a0[l], rw_k_k[l], rw_k_a[l], rw_r_k[l], rw_lnx_w[l], rw_lnx_b[l])
        x = _merge(y_a, y_b, y_c, p, x, gt1, w_branch[l].astype(BF16), w_out[l].astype(BF16))
        x = _ffn(x, norm2_w[l].reshape(1, D), sh2, sc2, gt2,
                 ffn_w1[l].astype(BF16), ffn_w3[l].astype(BF16), ffn_w2[l].astype(BF16),
                 final_norm_w.reshape(1, D), l == L - 1)
    return x
```

```python
import functools
import math

import jax
import jax.numpy as jnp
import numpy as np
from jax import lax
from jax.experimental import pallas as pl
from jax.experimental.pallas import tpu as pltpu

F32 = jnp.float32
BF16 = jnp.bfloat16
I32 = jnp.int32

LANES = 128
VMEM_LIMIT = 56 * 1024 * 1024

D_MODEL = 1024
DEPTH = 4
A_HEADS, A_DK, A_DV = 4, 128, 128
A_CHUNK, A_SUB = 64, 16
B_HEADS, B_GROUPS, B_HD = 8, 2, 64
B_J = B_HEADS // B_GROUPS
CMP_LEN, CMP_STRIDE, CMP_HIDDEN = 32, 16, 256
SEL_LEN, SEL_TOP, N_LOCAL = 64, 16, 2
WINDOW, Q_BLOCK = 512, 128
N_WIN_TILES = WINDOW // Q_BLOCK + 1
SEL_TILES_PER_STEP = 4
REL_BUCKETS, REL_MAX_DIST = 32, 128
C_HEADS, C_HD = 8, 64
C_CHUNK = 64
C_SUBS = 4
C_DECAY_LORA, C_AAA_LORA, C_GATE_LORA = 64, 64, 128
LNX_EPS = 64e-5
D_FF = 2816

COL_A = 0
COL_GATE = 2048 // LANES
COL_C = 5120 // LANES
COL_B = 6912 // LANES
P_COLS = 8448
NSA_Q, NSA_CMP, NSA_SEL, NSA_WIN, NSA_GATE, NSA_BLOCKS = 0, 4, 6, 8, 10, 12
NSA_GATE_STRIDE = 16


def _cparams(sem):
    return pltpu.CompilerParams(dimension_semantics=sem, vmem_limit_bytes=VMEM_LIMIT)


def _split3(x):
    hi = x.astype(BF16)
    r1 = x - hi.astype(F32)
    mid = r1.astype(BF16)
    lo = (r1 - mid.astype(F32)).astype(BF16)
    return hi, mid, lo


def _mm(a, b):
    return jnp.dot(a.astype(BF16), b.astype(BF16), preferred_element_type=F32)


def _mm_nt(a, b):
    return lax.dot_general(a.astype(BF16), b.astype(BF16), (((1,), (1,)), ((), ())),
                           preferred_element_type=F32)


def _mm_tn(a, b):
    return lax.dot_general(a.astype(BF16), b.astype(BF16), (((0,), (0,)), ((), ())),
                           preferred_element_type=F32)


def _mm_exact_lhs(a_exact, b):
    a = a_exact.astype(BF16)
    hi, mid, lo = _split3(b)
    d = functools.partial(jnp.dot, preferred_element_type=F32)
    return d(a, hi) + d(a, mid) + d(a, lo)


def _mm_exact_rhs(a, b_exact):
    b = b_exact.astype(BF16)
    hi, mid, lo = _split3(a)
    d = functools.partial(jnp.dot, preferred_element_type=F32)
    return d(hi, b) + d(mid, b) + d(lo, b)


def _mm_x3(a, b):
    ah = a.astype(BF16)
    al = (a - ah.astype(F32)).astype(BF16)
    bh = b.astype(BF16)
    bl = (b - bh.astype(F32)).astype(BF16)
    d = functools.partial(jnp.dot, preferred_element_type=F32)
    return d(ah, bh) + d(ah, bl) + d(al, bh)


def _sigmoid(x):
    return 1.0 / (1.0 + jnp.exp(-x))


def _silu(x):
    return x * _sigmoid(x)


def _log_sigmoid(x):
    return jnp.minimum(x, 0.0) - jnp.log(1.0 + jnp.exp(-jnp.abs(x)))


def _iota(shape, dim):
    return lax.broadcasted_iota(I32, shape, dim)


def _ada_kernel(c_ref, w_ref, b_ref, o_ref):
    c = c_ref[...]
    o_ref[0] = _mm_x3(_silu(c), w_ref[0]) + b_ref[0]


def _ada(c_pad, ada_w, ada_b):
    L = ada_w.shape[0]
    n_out = ada_w.shape[2]
    tn = D_MODEL
    return pl.pallas_call(
        _ada_kernel,
        grid=(L, n_out // tn),
        in_specs=[pl.BlockSpec((8, D_MODEL), lambda l, j: (0, 0)),
                  pl.BlockSpec((1, D_MODEL, tn), lambda l, j: (l, 0, j)),
                  pl.BlockSpec((1, 1, tn), lambda l, j: (l, 0, j))],
        out_specs=pl.BlockSpec((1, 8, tn), lambda l, j: (l, 0, j)),
        out_shape=jax.ShapeDtypeStruct((L, 8, n_out), F32),
        compiler_params=_cparams(("arbitrary", "arbitrary")),
        name="ada_ln",
    )(c_pad, ada_w, ada_b.reshape(L, 1, n_out))


def _norm_mod(x, nw, sh, sc):
    y = x * lax.rsqrt(jnp.mean(x * x, axis=-1, keepdims=True) + 1e-6) * nw
    return y * (1.0 + sc) + sh


def _inproj_kernel(x_ref, nw_ref, sh_ref, sc_ref, w_ref, o_ref, h_scr):
    @pl.when(pl.program_id(2) == 0)
    def _():
        h_scr[...] = _norm_mod(x_ref[0], nw_ref[...], sh_ref[0], sc_ref[0]).astype(BF16)

    o_ref[0] = jnp.dot(h_scr[...], w_ref[...], preferred_element_type=F32)


def _inproj(x, nw, sh, sc, w):
    Bsz, T, D = x.shape
    n_out = w.shape[1]
    tm, tn = min(T, 1024), 768
    return pl.pallas_call(
        _inproj_kernel,
        grid=(Bsz, T // tm, n_out // tn),
        in_specs=[pl.BlockSpec((1, tm, D), lambda b, i, n: (b, i, 0)),
                  pl.BlockSpec((1, D), lambda b, i, n: (0, 0)),
                  pl.BlockSpec((1, 1, D), lambda b, i, n: (b, 0, 0)),
                  pl.BlockSpec((1, 1, D), lambda b, i, n: (b, 0, 0)),
                  pl.BlockSpec((D, tn), lambda b, i, n: (0, n))],
        out_specs=pl.BlockSpec((1, tm, tn), lambda b, i, n: (b, i, n)),
        out_shape=jax.ShapeDtypeStruct((Bsz, T, n_out), F32),
        scratch_shapes=[pltpu.VMEM((tm, D), BF16)],
        compiler_params=_cparams(("arbitrary", "arbitrary", "arbitrary")),
        name="in_proj",
    )(x, nw, sh, sc, w)


def _hgrn_kernel(layer, lbl_ref, nw_ref, q_ref, f_ref, i_ref, g_ref, o_ref, s_scr):
    C, c = A_CHUNK, A_SUB
    slabs = [slice(h * A_DK, (h + 1) * A_DK) for h in range(A_HEADS)]

    @pl.when(pl.program_id(1) == 0)
    def _():
        s_scr[...] = jnp.zeros_like(s_scr)

    logits = lbl_ref[...]
    ex = jnp.exp(logits - jnp.max(logits, axis=0, keepdims=True))
    sm = ex / jnp.sum(ex, axis=0, keepdims=True)
    lb = jnp.zeros((1, logits.shape[1]), F32)
    for i in range(1, layer + 1):
        lb = lb + sm[i:i + 1, :]

    q = _silu(q_ref[0])
    fr = f_ref[0]
    v = i_ref[0]
    la = jnp.log(lb)
    lc = jnp.log(1.0 - lb) + _log_sigmoid(fr)
    log_f = jnp.maximum(la, lc) + jnp.log(1.0 + jnp.exp(-jnp.abs(la - lc)))
    k = (1.0 - lb) * _sigmoid(-fr)

    tri = (_iota((C, C), 0) >= _iota((C, C), 1)).astype(BF16)
    b = _mm_exact_lhs(tri, log_f)
    b_last = b[C - 1:C, :]

    qe = q * jnp.exp(b)
    o = [_mm_nt(qe[:, sl], s_scr[h]) for h, sl in enumerate(slabs)]
    off = [[jnp.zeros((c, A_DV), F32)] for _ in slabs]
    for blk in range(1, C // c):
        lo = blk * c
        ref = b[lo - 1:lo, :]
        qi = q[lo:lo + c] * jnp.exp(b[lo:lo + c] - ref)
        ki = k[:lo] * jnp.exp(ref - b[:lo])
        att = [_mm_nt(qi[:, sl], ki[:, sl]) for sl in slabs]
        for h, sl in enumerate(slabs):
            off[h].append(_mm(att[h], v[:lo, sl]))
    o = [o[h] + jnp.concatenate(off[h], axis=0) for h in range(len(slabs))]
    row = _iota((C, 1), 0) % c
    for j in range(c):
        if j == 0:
            kj, bj, vj = k, b, v
        else:
            kj, bj, vj = (pltpu.roll(t, j, 0) for t in (k, b, v))
        ok = row >= j
        term = jnp.where(ok, q * kj * jnp.exp(jnp.where(ok, b - bj, 0.0)), 0.0)
        for h, sl in enumerate(slabs):
            o[h] = o[h] + jnp.sum(term[:, sl], axis=1, keepdims=True) * vj[:, sl]

    kd = k * jnp.exp(b_last - b)
    p_end = jnp.exp(b_last)
    for h, sl in enumerate(slabs):
        s_scr[h] = s_scr[h] * p_end[:, sl] + _mm_tn(v[:, sl], kd[:, sl])

    gate = _sigmoid(g_ref[0])
    for h, sl in enumerate(slabs):
        oh = o[h]
        oh = oh * lax.rsqrt(jnp.mean(oh * oh, axis=-1, keepdims=True) + 1e-5) * nw_ref[:, sl]
        o_ref[0, :, sl] = oh * gate[:, sl]


def _hgrn(p, lb_logits, norm_w, layer):
    Bsz, T, _ = p.shape
    C = A_CHUNK
    L = lb_logits.shape[0]
    W = A_HEADS * A_DK
    cw = COL_A * LANES // W

    def col(i):
        return pl.BlockSpec((1, C, W), lambda b, c: (b, c, cw + i))

    return pl.pallas_call(
        functools.partial(_hgrn_kernel, layer),
        grid=(Bsz, T // C),
        in_specs=[pl.BlockSpec((L, W), lambda b, c: (0, 0)),
                  pl.BlockSpec((1, A_HEADS * A_DV), lambda b, c: (0, 0)),
                  col(0), col(1), col(2), col(3)],
        out_specs=pl.BlockSpec((1, C, A_HEADS * A_DV), lambda b, c: (b, c, 0)),
        out_shape=jax.ShapeDtypeStruct((Bsz, T, A_HEADS * A_DV), F32),
        scratch_shapes=[pltpu.VMEM((A_HEADS, A_DV, A_DK), F32)],
        compiler_params=_cparams(("arbitrary", "arbitrary")),
        name="hgrn2",
    )(lb_logits, norm_w.reshape(1, -1), p, p, p, p)


def _rwkv_kernel(r_ref, k_ref, v_ref, wa_ref, xg_ref, mu_ref,
                 w2_ref, a2_ref, g2_ref, w0_ref, a0_ref, kk_ref, ka_ref, rk_ref, lnw_ref, lnb_ref,
                 o_ref, s_scr, pr_scr, pk_scr, pv_scr, pwa_scr, pg_scr):
    C = C_CHUNK
    W = C_HEADS * C_HD
    n_hp = W // LANES

    @pl.when(pl.program_id(1) == 0)
    def _():
        s_scr[...] = jnp.zeros_like(s_scr)
        for scr in (pr_scr, pk_scr, pv_scr, pwa_scr, pg_scr):
            scr[...] = jnp.zeros_like(scr)

    R = C_SUBS * C
    subs = [slice(s * C, (s + 1) * C) for s in range(C_SUBS)]
    row0 = _iota((R, 1), 0) == 0

    def shifted(ref, prev_scr, mu):
        cur = ref[0]
        prev = jnp.where(row0, prev_scr[...], pltpu.roll(cur, 1, 0))
        prev_scr[...] = cur[R - 1:R, :]
        return cur + (prev - cur) * mu

    r = shifted(r_ref, pr_scr, mu_ref[:, 0:W])
    k = shifted(k_ref, pk_scr, mu_ref[:, W:2 * W])
    v = shifted(v_ref, pv_scr, mu_ref[:, 2 * W:3 * W])
    xwa = shifted(wa_ref, pwa_scr, mu_ref[:, 3 * W:3 * W + LANES])
    xg = shifted(xg_ref, pg_scr, mu_ref[:, 3 * W + LANES:])

    w = w0_ref[...] + _mm_x3(jnp.tanh(xwa), w2_ref[...])
    sp = jnp.maximum(-w, 0.0) + jnp.log(1.0 + jnp.exp(-jnp.abs(w)))
    lw = -jnp.exp(-sp - 0.5)
    a = _sigmoid(a0_ref[...] + _mm_x3(xwa, a2_ref[...]))
    g = _mm_x3(_sigmoid(xg), g2_ref[...])

    ri, rj = _iota((R, R), 0), _iota((R, R), 1)
    tri_incl = ((ri >= rj) & (ri // C == rj // C)).astype(BF16)
    b = _mm_exact_lhs(tri_incl, lw)
    b_last = [b[rs.stop - 1:rs.stop, :] for rs in subs]
    p_end = [jnp.exp(bl) for bl in b_last]
    inv_p = jnp.exp(-b)
    e_end = jnp.concatenate([jnp.exp(bl - b[rs]) for bl, rs in zip(b_last, subs)], axis=0)
    kk_raw = k * kk_ref[...]
    k = k * (1.0 + (a - 1.0) * ka_ref[...])
    bonus_in = r * k * rk_ref[...]
    r_t = r * jnp.exp(b)
    p_prev = jnp.exp(b - lw)

    head0 = _iota((1, LANES), 1) < C_HD
    bd_f = ((_iota((LANES, LANES), 0) < C_HD) == (_iota((LANES, LANES), 1) < C_HD)).astype(F32)
    ti = _iota((2 * C, 2 * C), 0) % C
    tj = _iota((2 * C, 2 * C), 1) % C
    lower_strict = ti > tj
    lower_incl = ti >= tj

    slabs = [slice(hp * LANES, (hp + 1) * LANES) for hp in range(n_hp)]
    dot = functools.partial(jnp.dot, preferred_element_type=F32)
    bd_b = bd_f.astype(BF16)

    def stack(x):
        return jnp.concatenate([jnp.where(head0, x, 0.0), jnp.where(head0, 0.0, x)], axis=0)

    def group_sums(x):
        st = jnp.concatenate([x[:, sl] for sl in slabs], axis=0)
        hi = st.astype(BF16)
        lo = (st - hi.astype(F32)).astype(BF16)
        out = dot(hi, bd_b) + dot(lo, bd_b)
        return jnp.concatenate([out[i * R:(i + 1) * R] for i in range(n_hp)], axis=1)

    kk = kk_raw * lax.rsqrt(jnp.maximum(group_sums(kk_raw * kk_raw), 1e-24))
    al = a * kk
    bonus = group_sums(bonus_in) * v
    kap = kk * p_prev
    al_h = al * inv_p
    k_h = k * inv_p
    al_e = al * e_end
    k_e = k * e_end

    C2 = 2 * C
    units = [(rs, sl) for rs in subs for sl in slabs]
    lhs = [jnp.concatenate([stack(kap[rs, sl]), stack(r_t[rs, sl])], axis=0) for rs, sl in units]
    v_st = [stack(v[rs, sl]) for rs, sl in units]
    e_st = [jnp.concatenate([stack(al_e[rs, sl]), stack(k_e[rs, sl])], axis=0) for rs, sl in units]
    big = [_mm_nt(lhs[i], jnp.concatenate([stack(al_h[rs, sl]), stack(k_h[rs, sl])], axis=0))
           for i, (rs, sl) in enumerate(units)]
    a_ak = [jnp.where(lower_strict, m[:C2, :C2], 0.0) for m in big]
    b_al = [jnp.where(lower_incl, m[C2:, :C2], 0.0) for m in big]
    kv = [_mm(jnp.concatenate([jnp.where(lower_strict, m[:C2, C2:], 0.0),
                               jnp.where(lower_incl, m[C2:, C2:], 0.0)], axis=0), v_st[i])
          for i, m in enumerate(big)]

    eye = (_iota((C2, C2), 0) == _iota((C2, C2), 1)).astype(F32)
    n = [-m for m in a_ak]
    x = [eye + m for m in n]
    n = [_mm(m, m) for m in n]
    power = 2
    while 2 * power < C:
        for i in range(len(units)):
            res = _mm(jnp.concatenate([x[i], n[i]], axis=0), n[i])
            x[i] = x[i] + res[:C2]
            n[i] = res[C2:]
        power *= 2
    x = [x[i] + _mm(x[i], n[i]) for i in range(len(units))]

    state = [s_scr[hp] for hp in range(n_hp)]
    y_rows = []
    for si in range(C_SUBS):
        y_parts = []
        for hp, sl in enumerate(slabs):
            i = si * n_hp + hp
            ks = _mm_nt(lhs[i], state[hp])
            u_st = -_mm(x[i], ks[:C2] + kv[i][:C2])
            y_st = ks[C2:] + _mm(b_al[i], u_st) + kv[i][C2:]
            y_parts.append(y_st[:C] + y_st[C:])
            state[hp] = state[hp] * p_end[si][:, sl] + _mm_tn(jnp.concatenate([u_st, v_st[i]], axis=0), e_st[i])
        y_rows.append(jnp.concatenate(y_parts, axis=1))
    for hp in range(n_hp):
        s_scr[hp] = state[hp]

    y = jnp.concatenate(y_rows, axis=0)
    inv_hd = 1.0 / C_HD
    yc = y - group_sums(y) * inv_hd
    var = group_sums(yc * yc) * inv_hd
    y = yc * lax.rsqrt(var + LNX_EPS) * lnw_ref[...] + lnb_ref[...]
    o_ref[0] = (y + bonus) * g


def _rwkv(p, mu, w2p, a2p, g2, w0, a0, k_k, k_a, r_k, lnx_w, lnx_b):
    Bsz, T, _ = p.shape
    C = C_SUBS * C_CHUNK
    W = C_HEADS * C_HD
    n_hp = W // LANES
    cw = COL_C * LANES // W

    def wide(i):
        return pl.BlockSpec((1, C, W), lambda b, c: (b, c, cw + i))

    def narrow(i):
        return pl.BlockSpec((1, C, LANES), lambda b, c: (b, c, COL_C + 3 * n_hp + i))

    def const(shape):
        return pl.BlockSpec(shape, lambda b, c: (0,) * len(shape))

    row = lambda t: t.reshape(1, -1)
    vecs = [row(t) for t in (w0, a0, k_k, k_a, r_k, lnx_w, lnx_b)]
    return pl.pallas_call(
        _rwkv_kernel,
        grid=(Bsz, T // C),
        in_specs=[wide(0), wide(1), wide(2), narrow(0), narrow(1), const((1, mu.shape[0])),
                  const(w2p.shape), const(a2p.shape), const(g2.shape)] + [const((1, W))] * 7,
        out_specs=pl.BlockSpec((1, C, W), lambda b, c: (b, c, 0)),
        out_shape=jax.ShapeDtypeStruct((Bsz, T, W), F32),
        scratch_shapes=[pltpu.VMEM((n_hp, LANES, LANES), F32)] + [pltpu.VMEM((1, W), F32)] * 3
                       + [pltpu.VMEM((1, LANES), F32)] * 2,
        compiler_params=_cparams(("arbitrary", "arbitrary")),
        name="rwkv7",
    )(p, p, p, p, p, row(mu), w2p, a2p, g2, *vecs)


def _t5_thresholds():
    max_exact = REL_BUCKETS // 2
    out = []
    for kk in range(1, REL_BUCKETS - max_exact):
        n = max_exact
        while int(math.log(n / max_exact) / math.log(REL_MAX_DIST / max_exact) * (REL_BUCKETS - max_exact)) < kk:
            n += 1
        out.append(n)
    return out


def _bias_from_dist(dist, rb_ref, head):
    n = jnp.maximum(dist, 0)
    max_exact = REL_BUCKETS // 2
    large = jnp.full(n.shape, max_exact, I32)
    for thr in _t5_thresholds():
        large = large + (n >= thr).astype(I32)
    bucket = jnp.where(n < max_exact, n, large)
    out = jnp.zeros(n.shape, F32)
    for j in range(REL_BUCKETS):
        out = jnp.where(bucket == j, rb_ref[j, head], out)
    return out


def _bias_cmp_kernel(rb_ref, o_ref):
    g, qb = pl.program_id(0), pl.program_id(1)
    n_pad = o_ref.shape[1]
    t = qb * Q_BLOCK + _iota((n_pad, Q_BLOCK), 1)
    cmp_end = _iota((n_pad, Q_BLOCK), 0) * CMP_STRIDE + CMP_LEN - 1
    for j in range(B_J):
        o_ref[0, :, j * Q_BLOCK:(j + 1) * Q_BLOCK] = _bias_from_dist(t - cmp_end, rb_ref, g * B_J + j)


def _bias_win_kernel(rb_ref, o_ref):
    g, d = pl.program_id(0), pl.program_id(1)
    dist = d * Q_BLOCK + _iota((Q_BLOCK, Q_BLOCK), 1) - _iota((Q_BLOCK, Q_BLOCK), 0)
    for j in range(B_J):
        o_ref[0, 0, :, j * Q_BLOCK:(j + 1) * Q_BLOCK] = _bias_from_dist(dist, rb_ref, g * B_J + j)


def _bias_tables(rel_bias, T, n_pad):
    smem = pl.BlockSpec(memory_space=pltpu.SMEM)
    jq = B_J * Q_BLOCK
    n_q = T // Q_BLOCK
    bias_c = pl.pallas_call(
        _bias_cmp_kernel,
        grid=(B_GROUPS, n_q),
        in_specs=[smem],
        out_specs=pl.BlockSpec((1, n_pad, jq), lambda g, i: (g, 0, i)),
        out_shape=jax.ShapeDtypeStruct((B_GROUPS, n_pad, n_q * jq), F32),
        compiler_params=_cparams(("arbitrary", "arbitrary")),
        name="nsa_bias_cmp",
    )(rel_bias)
    bias_w = pl.pallas_call(
        _bias_win_kernel,
        grid=(B_GROUPS, N_WIN_TILES),
        in_specs=[smem],
        out_specs=pl.BlockSpec((1, 1, Q_BLOCK, jq), lambda g, d: (g, d, 0, 0)),
        out_shape=jax.ShapeDtypeStruct((B_GROUPS, N_WIN_TILES, Q_BLOCK, jq), F32),
        compiler_params=_cparams(("arbitrary", "arbitrary")),
        name="nsa_bias_win",
    )(rel_bias)
    return bias_c, bias_w


def _compress_kernel(x_ref, pe_ref, w1_ref, w2_ref, o_ref):
    n_grp = o_ref.shape[2]
    hidden = w1_ref.shape[2]
    ha = jnp.zeros((n_grp, hidden), F32)
    hb = jnp.zeros((n_grp, hidden), F32)
    for tau in range(CMP_STRIDE):
        x_tau = x_ref[0, pl.ds(tau, n_grp, stride=CMP_STRIDE), :]
        ha = ha + _mm(x_tau + pe_ref[tau:tau + 1, :], w1_ref[tau])
        hb = hb + _mm(x_tau + pe_ref[CMP_STRIDE + tau:CMP_STRIDE + tau + 1, :], w1_ref[CMP_STRIDE + tau])
    h = ha + pltpu.roll(hb, n_grp - 1, 0)
    out = _mm(_silu(h), w2_ref[...])
    o_ref[0, 0] = jnp.where(_iota(out.shape, 0) < n_grp - 1, out, 0.0)


def _compress(p, pe, w1, w2):
    Bsz, T, _ = p.shape
    n_grp = T // CMP_STRIDE
    return pl.pallas_call(
        _compress_kernel,
        grid=(Bsz, B_GROUPS),
        in_specs=[pl.BlockSpec((1, T, LANES), lambda b, g: (b, 0, COL_B + NSA_CMP + g)),
                  pl.BlockSpec(pe.shape, lambda b, g: (0, 0)),
                  pl.BlockSpec(w1.shape, lambda b, g: (0, 0, 0)),
                  pl.BlockSpec(w2.shape, lambda b, g: (0, 0))],
        out_specs=pl.BlockSpec((1, 1, n_grp, LANES), lambda b, g: (b, g, 0, 0)),
        out_shape=jax.ShapeDtypeStruct((Bsz, B_GROUPS, n_grp, LANES), F32),
        compiler_params=_cparams(("arbitrary", "arbitrary")),
        name="nsa_compress",
    )(p, pe, w1, w2)


def _nsa_kernel(q_ref, gate_ref, kvc_ref, ks_ref, kw_ref, bc_ref, bw_ref, cov_ref, o_ref,
                ks_scr, kw_scr, vs_scr, vw_scr, gate_scr):
    QB, J, HD = Q_BLOCK, B_J, B_HD
    JQ = J * QB
    grp = pl.program_id(1)
    qb = pl.program_id(2)
    t0 = qb * QB
    n_pad = kvc_ref.shape[2]
    n_tiles = ks_ref.shape[1] // QB
    NEG = -jnp.inf
    dot = functools.partial(jnp.dot, preferred_element_type=F32)

    @pl.when(qb == 0)
    def _():
        def fill(kt, carry):
            rows = pl.ds(pl.multiple_of(kt * QB, QB), QB)
            for src, k_scr, v_scr in ((ks_ref, ks_scr, vs_scr), (kw_ref, kw_scr, vw_scr)):
                tile = src[0, rows, :]
                k_scr[rows, :] = tile.astype(BF16)
                v_scr[kt] = tile.T[HD:, :].astype(BF16)
            return carry
        lax.fori_loop(0, n_tiles, fill, 0)

    q_rows = (q_ref[0] * (HD ** -0.5)).T
    q_t = jnp.concatenate([q_rows[j * HD:(j + 1) * HD] for j in range(J)], axis=1)
    q_t = jnp.concatenate([q_t, jnp.zeros_like(q_t)], axis=0).astype(BF16)
    i_lane = _iota((1, JQ), 1) % QB
    t_lane = t0 + i_lane
    t_q = t0 + _iota((1, QB), 1)

    def finish(e_sum):
        return 1.0 / jnp.maximum(e_sum, 1e-30)

    def softmax_parts(scores):
        m = functools.reduce(jnp.maximum, [jnp.max(sc, axis=0, keepdims=True) for sc in scores])
        m = jnp.where(m == NEG, 0.0, m)
        es = [jnp.exp(sc - m) for sc in scores]
        tot = functools.reduce(jnp.add, [jnp.sum(e, axis=0, keepdims=True) for e in es])
        return es, finish(tot)

    n_col = _iota((n_pad, 1), 0)
    mask_c = ((n_col * CMP_STRIDE + CMP_LEN - 1) <= t_lane) & (n_col < n_pad - 1)
    kvc = kvc_ref[0, 0].astype(BF16)
    s = jnp.where(mask_c, dot(kvc, q_t) + bc_ref[0], NEG)
    (e_c,), inv_c = softmax_parts([s])
    p_c = e_c * inv_c
    o_c = _mm_tn(kvc, p_c)[HD:]

    p_sum = p_c[:, 0:QB]
    for j in range(1, J):
        p_sum = p_sum + p_c[:, j * QB:(j + 1) * QB]
    ph = p_sum.astype(BF16)
    pl_ = (p_sum - ph.astype(F32)).astype(BF16)
    cov = cov_ref[...]
    imp = dot(cov, ph) + dot(cov, pl_)
    n_blk = cov.shape[0]
    m_col = _iota((n_blk, 1), 0)
    cur = t_q // SEL_LEN
    causal = m_col <= cur
    forced = (m_col == 0) | ((m_col > cur - N_LOCAL) & causal)
    score = jnp.where(causal, jnp.where(forced, jnp.inf, imp), NEG)
    n_real = ks_ref.shape[1] // SEL_LEN
    st = score[:n_real, :]
    blk = _iota((n_real, 1), 0)
    cnt = jnp.zeros((n_real, QB), F32)
    for mp in range(n_real):
        other = st[mp:mp + 1, :]
        beats = (other > st) | ((other == st) & (mp < blk))
        cnt = cnt + beats.astype(F32)
    sel_t = (cnt < float(SEL_TOP)).astype(BF16)
    if n_real < n_blk:
        sel_t = jnp.concatenate([sel_t, jnp.zeros((n_blk - n_real, QB), BF16)], axis=0)

    key_col = _iota((QB, 1), 0)
    blk_lane = _iota((1, n_blk), 1)
    blocks_per_tile = QB // SEL_LEN

    def sel_scores(kt):
        ktc = jnp.minimum(kt, n_tiles - 1)
        start = pl.multiple_of(ktc * QB, QB)
        k_t = ks_scr[pl.ds(start, QB), :]
        expand = ((key_col // SEL_LEN + kt * blocks_per_tile) == blk_lane).astype(BF16)
        member = dot(expand, sel_t) > 0.5
        ok = member & ((kt * QB + key_col) <= t_q)
        ok = jnp.concatenate([ok] * J, axis=1)
        sc = dot(k_t, q_t) + bw_ref[0, jnp.clip(qb - kt, 0, 2)]
        return jnp.where(ok, sc, NEG), vs_scr[ktc]

    def sel_step(it, carry):
        m_old, l_old, acc = carry
        tiles = [sel_scores(it * SEL_TILES_PER_STEP + u) for u in range(SEL_TILES_PER_STEP)]
        m_new = functools.reduce(jnp.maximum, [m_old] + [jnp.max(sc, axis=0, keepdims=True) for sc, _ in tiles])
        m_use = jnp.where(m_new == NEG, 0.0, m_new)
        alpha = jnp.exp(m_old - m_use)
        l_new = alpha * l_old
        acc = alpha * acc
        for sc, v_t in tiles:
            pr = jnp.exp(sc - m_use)
            l_new = l_new + jnp.sum(pr, axis=0, keepdims=True)
            acc = acc + dot(v_t, pr.astype(BF16))
        return m_new, l_new, acc

    init = (jnp.full((1, JQ), NEG, F32), jnp.zeros((1, JQ), F32), jnp.zeros((HD, JQ), F32))
    n_steps = (qb + SEL_TILES_PER_STEP) // SEL_TILES_PER_STEP
    _, l_s, acc_s = lax.fori_loop(0, n_steps, sel_step, init)
    o_s = acc_s * finish(l_s)

    scores, values = [], []
    for d in range(N_WIN_TILES):
        kt = jnp.maximum(qb - d, 0)
        start = pl.multiple_of(kt * QB, QB)
        sc = dot(kw_scr[pl.ds(start, QB), :], q_t) + bw_ref[0, d]
        ok = qb >= d
        if d == 0:
            ok = ok & (key_col <= i_lane)
        elif d == N_WIN_TILES - 1:
            ok = ok & (key_col > i_lane)
        else:
            ok = jnp.broadcast_to(ok, (QB, JQ))
        scores.append(jnp.where(ok, sc, NEG))
        values.append(vw_scr[kt])
    es, inv_w = softmax_parts(scores)
    acc_w = functools.reduce(jnp.add, [dot(v_t, e.astype(BF16)) for v_t, e in zip(values, es)])
    o_w = acc_w * inv_w

    gate_scr[...] = _sigmoid(gate_ref[0]).T
    g_rows = gate_scr[pl.ds(pl.multiple_of(grp * NSA_GATE_STRIDE, NSA_GATE_STRIDE), NSA_GATE_STRIDE), :]
    g_c, g_s, g_w = (jnp.concatenate([g_rows[r * J + j:r * J + j + 1, :] for j in range(J)], axis=1)
                     for r in range(3))
    o_t = g_c * o_c + g_s * o_s + g_w * o_w
    o_ref[0] = jnp.concatenate([o_t[:, j * QB:(j + 1) * QB] for j in range(J)], axis=0).T


def _nsa(p, kvc, bias_c, bias_w, cover_t):
    Bsz, T, _ = p.shape
    G, J, HD, QB = B_GROUPS, B_J, B_HD, Q_BLOCK
    n_q = T // QB
    n_pad = kvc.shape[2]
    JQ = J * QB
    q_blk = COL_B * LANES // (J * HD)

    def kv_rows(first):
        return pl.BlockSpec((1, T, LANES), lambda b, g, i: (b, 0, COL_B + first + g))

    return pl.pallas_call(
        _nsa_kernel,
        grid=(Bsz, G, n_q),
        in_specs=[pl.BlockSpec((1, QB, J * HD), lambda b, g, i: (b, i, q_blk + g)),
                  pl.BlockSpec((1, QB, LANES), lambda b, g, i: (b, i, COL_B + NSA_GATE)),
                  pl.BlockSpec((1, 1, n_pad, LANES), lambda b, g, i: (b, g, 0, 0)),
                  kv_rows(NSA_SEL), kv_rows(NSA_WIN),
                  pl.BlockSpec((1, n_pad, JQ), lambda b, g, i: (g, 0, i)),
                  pl.BlockSpec((1, N_WIN_TILES, QB, JQ), lambda b, g, i: (g, 0, 0, 0)),
                  pl.BlockSpec(cover_t.shape, lambda b, g, i: (0, 0))],
        out_specs=pl.BlockSpec((1, QB, J * HD), lambda b, g, i: (b, i, g)),
        out_shape=jax.ShapeDtypeStruct((Bsz, T, G * J * HD), F32),
        scratch_shapes=[pltpu.VMEM((T, LANES), BF16), pltpu.VMEM((T, LANES), BF16),
                        pltpu.VMEM((n_q, HD, QB), BF16), pltpu.VMEM((n_q, HD, QB), BF16),
                        pltpu.VMEM((LANES, QB), F32)],
        compiler_params=_cparams(("arbitrary", "arbitrary", "arbitrary")),
        name="nsa_attention",
    )(p, p, kvc, p, p, bias_c, bias_w, cover_t)


def _merge_kernel(ya_ref, yb_ref, yc_ref, ga_ref, gb_ref, gc_ref, x_ref, gt_ref, wb_ref, wo_ref, o_ref):
    merged = (_sigmoid(ga_ref[0]) * _mm(ya_ref[0], wb_ref[0])
              + _sigmoid(gb_ref[0]) * _mm(yb_ref[0], wb_ref[1])
              + _sigmoid(gc_ref[0]) * _mm(yc_ref[0], wb_ref[2]))
    o_ref[0] = x_ref[0] + gt_ref[0] * _mm(merged, wo_ref[...])


def _merge(ya, yb, yc, p, x, gt, wb, wo):
    Bsz, T, D = x.shape
    tm = 512
    bw = ya.shape[2]
    gate_blk = COL_GATE * LANES // D
    yspec = pl.BlockSpec((1, tm, bw), lambda b, i: (b, i, 0))

    def gspec(k):
        return pl.BlockSpec((1, tm, D), lambda b, i: (b, i, gate_blk + k))

    return pl.pallas_call(
        _merge_kernel,
        grid=(Bsz, T // tm),
        in_specs=[yspec, yspec, yspec, gspec(0), gspec(1), gspec(2),
                  pl.BlockSpec((1, tm, D), lambda b, i: (b, i, 0)),
                  pl.BlockSpec((1, 1, D), lambda b, i: (b, 0, 0)),
                  pl.BlockSpec(wb.shape, lambda b, i: (0, 0, 0)),
                  pl.BlockSpec(wo.shape, lambda b, i: (0, 0))],
        out_specs=pl.BlockSpec((1, tm, D), lambda b, i: (b, i, 0)),
        out_shape=jax.ShapeDtypeStruct(x.shape, F32),
        compiler_params=_cparams(("arbitrary", "arbitrary")),
        name="merge_out_proj",
    )(ya, yb, yc, p, p, p, x, gt, wb, wo)


def _ffn_kernel(final, x_ref, nw_ref, sh_ref, sc_ref, gt_ref, w1_ref, w3_ref, w2_ref, fw_ref, o_ref,
                h_scr, acc_scr):
    f = pl.program_id(2)

    @pl.when(f == 0)
    def _():
        h_scr[...] = _norm_mod(x_ref[0], nw_ref[...], sh_ref[0], sc_ref[0]).astype(BF16)
        acc_scr[...] = jnp.zeros_like(acc_scr)

    h = h_scr[...]
    u = _silu(jnp.dot(h, w1_ref[...], preferred_element_type=F32)) * jnp.dot(h, w3_ref[...],
                                                                           preferred_element_type=F32)
    acc_scr[...] += _mm(u, w2_ref[...])

    @pl.when(f == pl.num_programs(2) - 1)
    def _():
        x = x_ref[0] + gt_ref[0] * acc_scr[...]
        if final:
            x = x * lax.rsqrt(jnp.mean(x * x, axis=-1, keepdims=True) + 1e-6) * fw_ref[...]
        o_ref[0] = x


def _ffn(x, nw, sh, sc, gt, w1, w3, w2, fw, final):
    Bsz, T, D = x.shape
    tm, tf = min(T, 1024), 256
    n_ff = w1.shape[1]
    vec = pl.BlockSpec((1, D), lambda b, i, f: (0, 0))
    bvec = pl.BlockSpec((1, 1, D), lambda b, i, f: (b, 0, 0))
    xspec = pl.BlockSpec((1, tm, D), lambda b, i, f: (b, i, 0))
    return pl.pallas_call(
        functools.partial(_ffn_kernel, final),
        grid=(Bsz, T // tm, n_ff // tf),
        in_specs=[xspec, vec, bvec, bvec, bvec,
                  pl.BlockSpec((D, tf), lambda b, i, f: (0, f)),
                  pl.BlockSpec((D, tf), lambda b, i, f: (0, f)),
                  pl.BlockSpec((tf, D), lambda b, i, f: (f, 0)),
                  vec],
        out_specs=xspec,
        out_shape=jax.ShapeDtypeStruct(x.shape, F32),
        scratch_shapes=[pltpu.VMEM((tm, D), BF16), pltpu.VMEM((tm, D), F32)],
        compiler_params=_cparams(("arbitrary", "arbitrary", "arbitrary")),
        name="ffn_swiglu",
    )(x, nw, sh, sc, gt, w1, w3, w2, fw)


def _pack_w_in(w):
    a_in, b_in, c_in = 2048, 1304, 1792
    wa = w[:, :a_in]
    wb = w[:, a_in:a_in + b_in]
    wc = w[:, a_in + b_in:a_in + b_in + c_in]
    wg = w[:, a_in + b_in + c_in:]
    src, used = _nsa_columns()
    wb = jnp.where(used[None, :], jnp.take(wb, src, axis=1), 0.0)
    return jnp.concatenate([wa, wg, wc, wb], axis=1).astype(BF16)


def _nsa_columns():
    G, J, HD = B_GROUPS, B_J, B_HD
    width = NSA_BLOCKS * LANES
    src = np.zeros((width,), np.int32)
    used = np.zeros((width,), bool)
    n_q = G * J * HD
    src[:n_q] = np.arange(n_q)
    used[:n_q] = True
    for pair, blk in enumerate((NSA_CMP, NSA_SEL, NSA_WIN)):
        for g in range(G):
            for kv in range(2):
                dst = (blk + g) * LANES + kv * HD
                src[dst:dst + HD] = n_q + (2 * pair + kv) * G * HD + g * HD + np.arange(HD)
                used[dst:dst + HD] = True
    gate0 = n_q + 6 * G * HD
    for g in range(G):
        for j in range(J):
            for r in range(3):
                dst = NSA_GATE * LANES + g * NSA_GATE_STRIDE + r * J + j
                src[dst] = gate0 + (g * J + j) * 3 + r
                used[dst] = True
    return src, used


def _nsa_layer(p, pe_k, w1_k, w2_k, pe_v, w1_v, w2_v, bias_c, bias_w, cover):
    HD = B_HD
    w1k = w1_k.reshape(CMP_LEN, HD, CMP_HIDDEN)
    w1v = w1_v.reshape(CMP_LEN, HD, CMP_HIDDEN)
    zero1 = jnp.zeros_like(w1k)
    w1 = jnp.concatenate([jnp.concatenate([w1k, zero1], axis=2),
                          jnp.concatenate([zero1, w1v], axis=2)], axis=1).astype(BF16)
    zero2 = jnp.zeros_like(w2_k)
    w2 = jnp.concatenate([jnp.concatenate([w2_k, zero2], axis=1),
                          jnp.concatenate([zero2, w2_v], axis=1)], axis=0).astype(BF16)
    pe = jnp.concatenate([pe_k, pe_v], axis=1)
    kvc = _compress(p, pe, w1, w2)
    return _nsa(p, kvc, bias_c, bias_w, cover)


def _cover_matrix(T, n_pad):
    n = jnp.arange(n_pad)[None, :] * CMP_STRIDE
    m = jnp.arange(LANES)[:, None] * SEL_LEN
    real = (jnp.arange(n_pad)[None, :] < (T - CMP_LEN) // CMP_STRIDE + 1) & (jnp.arange(LANES)[:, None] < T // SEL_LEN)
    return ((n < m + SEL_LEN) & (n + CMP_LEN > m) & real).astype(BF16)


def kernel(x, c, ada_w, ada_b, norm1_w, norm2_w, w_in, hgrn_lb_logits, hgrn_norm_w, nsa_pe_k, nsa_cmp_w1_k, nsa_cmp_w2_k, nsa_pe_v, nsa_cmp_w1_v, nsa_cmp_w2_v, rel_bias, rw_mu, rw_w0, rw_w2, rw_a0, rw_a2, rw_g2, rw_k_k, rw_k_a, rw_r_k, rw_lnx_w, rw_lnx_b, w_branch, w_out, ffn_w1, ffn_w3, ffn_w2, final_norm_w):
    Bsz, T, D = x.shape
    L = ada_w.shape[0]
    n_pad = T // CMP_STRIDE
    c_pad = jnp.pad(c, ((0, 8 - Bsz), (0, 0)))
    ada = _ada(c_pad, ada_w, ada_b)[:, :Bsz]
    bias_c, bias_w = _bias_tables(rel_bias, T, n_pad)
    cover = _cover_matrix(T, n_pad)
    zeros64 = jnp.zeros((C_DECAY_LORA, C_HEADS * C_HD), F32)
    for l in range(L):
        sh1, sc1, gt1, sh2, sc2, gt2 = (ada[l, :, None, i * D:(i + 1) * D] for i in range(6))
        p = _inproj(x, norm1_w[l].reshape(1, D), sh1, sc1, _pack_w_in(w_in[l]))
        y_a = _hgrn(p, hgrn_lb_logits, hgrn_norm_w[l], l)
        y_b = _nsa_layer(p, nsa_pe_k[l], nsa_cmp_w1_k[l], nsa_cmp_w2_k[l],
                         nsa_pe_v[l], nsa_cmp_w1_v[l], nsa_cmp_w2_v[l], bias_c, bias_w, cover)
        y_c = _rwkv(p, rw_mu[l],
                    jnp.concatenate([rw_w2[l], zeros64], axis=0),
                    jnp.concatenate([zeros64, rw_a2[l]], axis=0),
                    rw_g2[l], rw_w0[l], rw_a0[l], rw_k_k[l], rw_k_a[l], rw_r_k[l], rw_lnx_w[l], rw_lnx_b[l])
        x = _merge(y_a, y_b, y_c, p, x, gt1, w_branch[l].astype(BF16), w_out[l].astype(BF16))
        x = _ffn(x, norm2_w[l].reshape(1, D), sh2, sc2, gt2,
                 ffn_w1[l].astype(BF16), ffn_w3[l].astype(BF16), ffn_w2[l].astype(BF16),
                 final_norm_w.reshape(1, D), l == L - 1)
    return x
```

```python
import functools
import math

import jax
import jax.numpy as jnp
import numpy as np
from jax import lax
from jax.experimental import pallas as pl
from jax.experimental.pallas import tpu as pltpu

F32 = jnp.float32
BF16 = jnp.bfloat16
I32 = jnp.int32

LANES = 128
VMEM_LIMIT = 56 * 1024 * 1024

D_MODEL = 1024
DEPTH = 4
A_HEADS, A_DK, A_DV = 4, 128, 128
A_CHUNK, A_SUB = 64, 16
B_HEADS, B_GROUPS, B_HD = 8, 2, 64
B_J = B_HEADS // B_GROUPS
CMP_LEN, CMP_STRIDE, CMP_HIDDEN = 32, 16, 256
SEL_LEN, SEL_TOP, N_LOCAL = 64, 16, 2
WINDOW, Q_BLOCK = 512, 128
N_WIN_TILES = WINDOW // Q_BLOCK + 1
SEL_TILES_PER_STEP = 4
REL_BUCKETS, REL_MAX_DIST = 32, 128
LOG2E = math.log2(math.e)
C_HEADS, C_HD = 8, 64
C_CHUNK = 64
C_SUBS = 4
C_PACK = 4
C_DECAY_LORA, C_AAA_LORA, C_GATE_LORA = 64, 64, 128
LNX_EPS = 64e-5
D_FF = 2816

COL_A = 0
COL_GATE = 2048 // LANES
COL_C = 5120 // LANES
COL_B = 6912 // LANES
P_COLS = 8448
NSA_Q, NSA_CMP, NSA_SEL, NSA_WIN, NSA_GATE, NSA_BLOCKS = 0, 4, 6, 8, 10, 12
NSA_GATE_STRIDE = 16


def _cparams(sem):
    return pltpu.CompilerParams(dimension_semantics=sem, vmem_limit_bytes=VMEM_LIMIT)


def _split3(x):
    hi = x.astype(BF16)
    r1 = x - hi.astype(F32)
    mid = r1.astype(BF16)
    lo = (r1 - mid.astype(F32)).astype(BF16)
    return hi, mid, lo


def _mm(a, b):
    return jnp.dot(a.astype(BF16), b.astype(BF16), preferred_element_type=F32)


def _mm_nt(a, b):
    return lax.dot_general(a.astype(BF16), b.astype(BF16), (((1,), (1,)), ((), ())),
                           preferred_element_type=F32)


def _mm_tn(a, b):
    return lax.dot_general(a.astype(BF16), b.astype(BF16), (((0,), (0,)), ((), ())),
                           preferred_element_type=F32)


def _mm_exact_lhs(a_exact, b):
    a = a_exact.astype(BF16)
    hi, mid, lo = _split3(b)
    d = functools.partial(jnp.dot, preferred_element_type=F32)
    return d(a, hi) + d(a, mid) + d(a, lo)


def _mm_exact_rhs(a, b_exact):
    b = b_exact.astype(BF16)
    hi, mid, lo = _split3(a)
    d = functools.partial(jnp.dot, preferred_element_type=F32)
    return d(hi, b) + d(mid, b) + d(lo, b)


def _mm_x3(a, b):
    ah = a.astype(BF16)
    al = (a - ah.astype(F32)).astype(BF16)
    bh = b.astype(BF16)
    bl = (b - bh.astype(F32)).astype(BF16)
    d = functools.partial(jnp.dot, preferred_element_type=F32)
    return d(ah, bh) + d(ah, bl) + d(al, bh)


def _sigmoid(x):
    return 1.0 / (1.0 + jnp.exp(-x))


def _silu(x):
    return x * _sigmoid(x)


def _log_sigmoid(x):
    return jnp.minimum(x, 0.0) - jnp.log(1.0 + jnp.exp(-jnp.abs(x)))


def _iota(shape, dim):
    return lax.broadcasted_iota(I32, shape, dim)


def _ada_kernel(c_ref, w_ref, b_ref, o_ref):
    c = c_ref[...]
    o_ref[0] = _mm_x3(_silu(c), w_ref[0]) + b_ref[0]


def _ada(c_pad, ada_w, ada_b):
    L = ada_w.shape[0]
    n_out = ada_w.shape[2]
    tn = D_MODEL
    return pl.pallas_call(
        _ada_kernel,
        grid=(L, n_out // tn),
        in_specs=[pl.BlockSpec((8, D_MODEL), lambda l, j: (0, 0)),
                  pl.BlockSpec((1, D_MODEL, tn), lambda l, j: (l, 0, j)),
                  pl.BlockSpec((1, 1, tn), lambda l, j: (l, 0, j))],
        out_specs=pl.BlockSpec((1, 8, tn), lambda l, j: (l, 0, j)),
        out_shape=jax.ShapeDtypeStruct((L, 8, n_out), F32),
        compiler_params=_cparams(("arbitrary", "arbitrary")),
        name="ada_ln",
    )(c_pad, ada_w, ada_b.reshape(L, 1, n_out))


def _norm_mod(x, nw, sh, sc):
    y = x * lax.rsqrt(jnp.mean(x * x, axis=-1, keepdims=True) + 1e-6) * nw
    return y * (1.0 + sc) + sh


def _inproj_kernel(x_ref, nw_ref, sh_ref, sc_ref, w_ref, o_ref, h_scr):
    @pl.when(pl.program_id(2) == 0)
    def _():
        h_scr[...] = _norm_mod(x_ref[0], nw_ref[...], sh_ref[0], sc_ref[0]).astype(BF16)

    o_ref[0] = jnp.dot(h_scr[...], w_ref[...], preferred_element_type=F32)


def _inproj(x, nw, sh, sc, w):
    Bsz, T, D = x.shape
    n_out = w.shape[1]
    tm, tn = min(T, 1024), 768
    return pl.pallas_call(
        _inproj_kernel,
        grid=(Bsz, T // tm, n_out // tn),
        in_specs=[pl.BlockSpec((1, tm, D), lambda b, i, n: (b, i, 0)),
                  pl.BlockSpec((1, D), lambda b, i, n: (0, 0)),
                  pl.BlockSpec((1, 1, D), lambda b, i, n: (b, 0, 0)),
                  pl.BlockSpec((1, 1, D), lambda b, i, n: (b, 0, 0)),
                  pl.BlockSpec((D, tn), lambda b, i, n: (0, n))],
        out_specs=pl.BlockSpec((1, tm, tn), lambda b, i, n: (b, i, n)),
        out_shape=jax.ShapeDtypeStruct((Bsz, T, n_out), F32),
        scratch_shapes=[pltpu.VMEM((tm, D), BF16)],
        compiler_params=_cparams(("arbitrary", "arbitrary", "arbitrary")),
        name="in_proj",
    )(x, nw, sh, sc, w)


def _hgrn_kernel(layer, lbl_ref, nw_ref, q_ref, f_ref, i_ref, g_ref, o_ref, s_scr):
    C, c = A_CHUNK, A_SUB
    slabs = [slice(h * A_DK, (h + 1) * A_DK) for h in range(A_HEADS)]

    @pl.when(pl.program_id(1) == 0)
    def _():
        s_scr[...] = jnp.zeros_like(s_scr)

    logits = lbl_ref[...]
    ex = jnp.exp(logits - jnp.max(logits, axis=0, keepdims=True))
    sm = ex / jnp.sum(ex, axis=0, keepdims=True)
    lb = jnp.zeros((1, logits.shape[1]), F32)
    for i in range(1, layer + 1):
        lb = lb + sm[i:i + 1, :]

    q = _silu(q_ref[0])
    fr = f_ref[0]
    v = i_ref[0]
    la = jnp.log(lb)
    lc = jnp.log(1.0 - lb) + _log_sigmoid(fr)
    log_f = jnp.maximum(la, lc) + jnp.log(1.0 + jnp.exp(-jnp.abs(la - lc)))
    k = (1.0 - lb) * _sigmoid(-fr)

    tri = (_iota((C, C), 0) >= _iota((C, C), 1)).astype(BF16)
    b = _mm_exact_lhs(tri, log_f)
    b_last = b[C - 1:C, :]

    qe = q * jnp.exp(b)
    o = [_mm_nt(qe[:, sl], s_scr[h]) for h, sl in enumerate(slabs)]
    off = [[jnp.zeros((c, A_DV), F32)] for _ in slabs]
    for blk in range(1, C // c):
        lo = blk * c
        ref = b[lo - 1:lo, :]
        qi = q[lo:lo + c] * jnp.exp(b[lo:lo + c] - ref)
        ki = k[:lo] * jnp.exp(ref - b[:lo])
        att = [_mm_nt(qi[:, sl], ki[:, sl]) for sl in slabs]
        for h, sl in enumerate(slabs):
            off[h].append(_mm(att[h], v[:lo, sl]))
    o = [o[h] + jnp.concatenate(off[h], axis=0) for h in range(len(slabs))]
    row = _iota((C, 1), 0) % c
    for j in range(c):
        if j == 0:
            kj, bj, vj = k, b, v
        else:
            kj, bj, vj = (pltpu.roll(t, j, 0) for t in (k, b, v))
        ok = row >= j
        term = jnp.where(ok, q * kj * jnp.exp(jnp.where(ok, b - bj, 0.0)), 0.0)
        for h, sl in enumerate(slabs):
            o[h] = o[h] + jnp.sum(term[:, sl], axis=1, keepdims=True) * vj[:, sl]

    kd = k * jnp.exp(b_last - b)
    p_end = jnp.exp(b_last)
    for h, sl in enumerate(slabs):
        s_scr[h] = s_scr[h] * p_end[:, sl] + _mm_tn(v[:, sl], kd[:, sl])

    gate = _sigmoid(g_ref[0])
    for h, sl in enumerate(slabs):
        oh = o[h]
        oh = oh * lax.rsqrt(jnp.mean(oh * oh, axis=-1, keepdims=True) + 1e-5) * nw_ref[:, sl]
        o_ref[0, :, sl] = oh * gate[:, sl]


def _hgrn(p, lb_logits, norm_w, layer):
    Bsz, T, _ = p.shape
    C = A_CHUNK
    L = lb_logits.shape[0]
    W = A_HEADS * A_DK
    cw = COL_A * LANES // W

    def col(i):
        return pl.BlockSpec((1, C, W), lambda b, c: (b, c, cw + i))

    return pl.pallas_call(
        functools.partial(_hgrn_kernel, layer),
        grid=(Bsz, T // C),
        in_specs=[pl.BlockSpec((L, W), lambda b, c: (0, 0)),
                  pl.BlockSpec((1, A_HEADS * A_DV), lambda b, c: (0, 0)),
                  col(0), col(1), col(2), col(3)],
        out_specs=pl.BlockSpec((1, C, A_HEADS * A_DV), lambda b, c: (b, c, 0)),
        out_shape=jax.ShapeDtypeStruct((Bsz, T, A_HEADS * A_DV), F32),
        scratch_shapes=[pltpu.VMEM((A_HEADS, A_DV, A_DK), F32)],
        compiler_params=_cparams(("arbitrary", "arbitrary")),
        name="hgrn2",
    )(lb_logits, norm_w.reshape(1, -1), p, p, p, p)


def _rwkv_kernel(r_ref, k_ref, v_ref, wa_ref, xg_ref, mu_ref,
                 w2_ref, a2_ref, g2_ref, w0_ref, a0_ref, kk_ref, ka_ref, rk_ref, lnw_ref, lnb_ref,
                 o_ref, s_scr, pr_scr, pk_scr, pv_scr, pwa_scr, pg_scr):
    C = C_CHUNK
    W = C_HEADS * C_HD
    n_hp = W // LANES

    @pl.when(pl.program_id(1) == 0)
    def _():
        s_scr[...] = jnp.zeros_like(s_scr)
        for scr in (pr_scr, pk_scr, pv_scr, pwa_scr, pg_scr):
            scr[...] = jnp.zeros_like(scr)

    R = C_SUBS * C
    subs = [slice(s * C, (s + 1) * C) for s in range(C_SUBS)]
    row0 = _iota((R, 1), 0) == 0

    def shifted(ref, prev_scr, mu):
        cur = ref[0]
        prev = jnp.where(row0, prev_scr[...], pltpu.roll(cur, 1, 0))
        prev_scr[...] = cur[R - 1:R, :]
        return cur + (prev - cur) * mu

    r = shifted(r_ref, pr_scr, mu_ref[:, 0:W])
    k = shifted(k_ref, pk_scr, mu_ref[:, W:2 * W])
    v = shifted(v_ref, pv_scr, mu_ref[:, 2 * W:3 * W])
    xwa = shifted(wa_ref, pwa_scr, mu_ref[:, 3 * W:3 * W + LANES])
    xg = shifted(xg_ref, pg_scr, mu_ref[:, 3 * W + LANES:])

    w = w0_ref[...] + _mm_x3(jnp.tanh(xwa), w2_ref[...])
    sp = jnp.maximum(-w, 0.0) + jnp.log(1.0 + jnp.exp(-jnp.abs(w)))
    lw = -jnp.exp(-sp - 0.5)
    a = _sigmoid(a0_ref[...] + _mm_x3(xwa, a2_ref[...]))
    g = _mm_x3(_sigmoid(xg), g2_ref[...])

    ri, rj = _iota((R, R), 0), _iota((R, R), 1)
    tri_incl = ((ri >= rj) & (ri // C == rj // C)).astype(BF16)
    b = _mm_exact_lhs(tri_incl, lw)
    b_last = [b[rs.stop - 1:rs.stop, :] for rs in subs]
    p_end = [jnp.exp(bl) for bl in b_last]
    inv_p = jnp.exp(-b)
    e_end = jnp.concatenate([jnp.exp(bl - b[rs]) for bl, rs in zip(b_last, subs)], axis=0)
    kk_raw = k * kk_ref[...]
    k = k * (1.0 + (a - 1.0) * ka_ref[...])
    bonus_in = r * k * rk_ref[...]
    r_t = r * jnp.exp(b)
    p_prev = jnp.exp(b - lw)

    bd_f = ((_iota((LANES, LANES), 0) < C_HD) == (_iota((LANES, LANES), 1) < C_HD)).astype(F32)
    slabs = [slice(hp * LANES, (hp + 1) * LANES) for hp in range(n_hp)]
    dot = functools.partial(jnp.dot, preferred_element_type=F32)
    bd_b = bd_f.astype(BF16)

    PH = C_PACK
    PW = PH * C_HD
    n_packs = W // PW
    packs = [slice(i * PW, (i + 1) * PW) for i in range(n_packs)]
    s_lane = _iota((C, PH * C), 1) % C
    t_row = _iota((C, PH * C), 0)
    lower_strict = t_row > s_lane
    lower_incl = t_row >= s_lane
    same_head = (_iota((PH * C, PW), 0) // C) == (_iota((PH * C, PW), 1) // C_HD)
    state_mask = (_iota((PW, PW), 0) // C_HD) == (_iota((PW, PW), 1) // C_HD)

    def spread(x):
        xb = x.astype(BF16)
        return jnp.where(same_head, jnp.concatenate([xb] * PH, axis=0), jnp.zeros((), BF16))

    def group_sums(x):
        st = jnp.concatenate([x[:, sl] for sl in slabs], axis=0)
        hi = st.astype(BF16)
        lo = (st - hi.astype(F32)).astype(BF16)
        out = dot(hi, bd_b) + dot(lo, bd_b)
        return jnp.concatenate([out[i * R:(i + 1) * R] for i in range(n_hp)], axis=1)

    kk = kk_raw * lax.rsqrt(jnp.maximum(group_sums(kk_raw * kk_raw), 1e-24))
    al = a * kk
    bonus = group_sums(bonus_in) * v
    kap = kk * p_prev
    al_h = al * inv_p
    k_h = k * inv_p
    al_e = al * e_end
    k_e = k * e_end

    PC = PH * C
    units = [(rs, pk) for rs in subs for pk in packs]
    lhs = [jnp.concatenate([kap[rs, pk], r_t[rs, pk]], axis=0).astype(BF16) for rs, pk in units]
    e_st = [jnp.concatenate([al_e[rs, pk], k_e[rs, pk]], axis=0).astype(BF16) for rs, pk in units]
    v_b = [v[rs, pk].astype(BF16) for rs, pk in units]
    v_sp = [spread(v[rs, pk]) for rs, pk in units]
    big = [_mm_nt(lhs[i], jnp.concatenate([spread(al_h[rs, pk]), spread(k_h[rs, pk])], axis=0))
           for i, (rs, pk) in enumerate(units)]
    a_ak = [jnp.where(lower_strict, m[:C, :PC], 0.0) for m in big]
    b_al = [jnp.where(lower_incl, m[C:, :PC], 0.0) for m in big]
    kv = [dot(jnp.concatenate([jnp.where(lower_strict, m[:C, PC:], 0.0),
                               jnp.where(lower_incl, m[C:, PC:], 0.0)], axis=0).astype(BF16), v_sp[i])
          for i, m in enumerate(big)]

    eye = (t_row == s_lane).astype(F32)
    n = [-m for m in a_ak]
    x = [eye + m for m in n]
    n = [dot(m.astype(BF16), spread(m)) for m in n]
    power = 2
    while 2 * power < C:
        for i in range(len(units)):
            res = dot(jnp.concatenate([x[i], n[i]], axis=0).astype(BF16), spread(n[i]))
            x[i] = x[i] + res[:C]
            n[i] = res[C:]
        power *= 2
    x = [x[i] + dot(x[i].astype(BF16), spread(n[i])) for i in range(len(units))]

    state = [s_scr[i] for i in range(n_packs)]
    y_rows = []
    for si in range(C_SUBS):
        y_parts = []
        for pi, pk in enumerate(packs):
            i = si * n_packs + pi
            ks = _mm_nt(lhs[i], state[pi])
            u = -dot(x[i].astype(BF16), spread(ks[:C] + kv[i][:C]))
            y_parts.append(ks[C:] + dot(b_al[i].astype(BF16), spread(u)) + kv[i][C:])
            upd = _mm_tn(jnp.concatenate([u.astype(BF16), v_b[i]], axis=0), e_st[i])
            state[pi] = state[pi] * p_end[si][:, pk] + jnp.where(state_mask, upd, 0.0)
        y_rows.append(jnp.concatenate(y_parts, axis=1))
    for pi in range(n_packs):
        s_scr[pi] = state[pi]

    y = jnp.concatenate(y_rows, axis=0)
    inv_hd = 1.0 / C_HD
    yc = y - group_sums(y) * inv_hd
    var = group_sums(yc * yc) * inv_hd
    y = yc * lax.rsqrt(var + LNX_EPS) * lnw_ref[...] + lnb_ref[...]
    o_ref[0] = (y + bonus) * g


def _rwkv(p, mu, w2p, a2p, g2, w0, a0, k_k, k_a, r_k, lnx_w, lnx_b):
    Bsz, T, _ = p.shape
    C = C_SUBS * C_CHUNK
    W = C_HEADS * C_HD
    n_hp = W // LANES
    cw = COL_C * LANES // W
    pw = C_PACK * C_HD
    assert C_CHUNK == C_HD and W % pw == 0

    def wide(i):
        return pl.BlockSpec((1, C, W), lambda b, c: (b, c, cw + i))

    def narrow(i):
        return pl.BlockSpec((1, C, LANES), lambda b, c: (b, c, COL_C + 3 * n_hp + i))

    def const(shape):
        return pl.BlockSpec(shape, lambda b, c: (0,) * len(shape))

    row = lambda t: t.reshape(1, -1)
    vecs = [row(t) for t in (w0, a0, k_k, k_a, r_k, lnx_w, lnx_b)]
    return pl.pallas_call(
        _rwkv_kernel,
        grid=(Bsz, T // C),
        in_specs=[wide(0), wide(1), wide(2), narrow(0), narrow(1), const((1, mu.shape[0])),
                  const(w2p.shape), const(a2p.shape), const(g2.shape)] + [const((1, W))] * 7,
        out_specs=pl.BlockSpec((1, C, W), lambda b, c: (b, c, 0)),
        out_shape=jax.ShapeDtypeStruct((Bsz, T, W), F32),
        scratch_shapes=[pltpu.VMEM((W // pw, pw, pw), F32)] + [pltpu.VMEM((1, W), F32)] * 3
                       + [pltpu.VMEM((1, LANES), F32)] * 2,
        compiler_params=_cparams(("arbitrary", "arbitrary")),
        name="rwkv7",
    )(p, p, p, p, p, row(mu), w2p, a2p, g2, *vecs)


def _t5_thresholds():
    max_exact = REL_BUCKETS // 2
    out = []
    for kk in range(1, REL_BUCKETS - max_exact):
        n = max_exact
        while int(math.log(n / max_exact) / math.log(REL_MAX_DIST / max_exact) * (REL_BUCKETS - max_exact)) < kk:
            n += 1
        out.append(n)
    return out


def _bias_from_dist(dist, rb_ref, head):
    n = jnp.maximum(dist, 0)
    max_exact = REL_BUCKETS // 2
    large = jnp.full(n.shape, max_exact, I32)
    for thr in _t5_thresholds():
        large = large + (n >= thr).astype(I32)
    bucket = jnp.where(n < max_exact, n, large)
    out = jnp.zeros(n.shape, F32)
    for j in range(REL_BUCKETS):
        out = jnp.where(bucket == j, rb_ref[j, head] * LOG2E, out)
    return out


def _bias_cmp_kernel(rb_ref, o_ref):
    g, qb = pl.program_id(0), pl.program_id(1)
    n_pad = o_ref.shape[1]
    t = qb * Q_BLOCK + _iota((n_pad, Q_BLOCK), 1)
    cmp_end = _iota((n_pad, Q_BLOCK), 0) * CMP_STRIDE + CMP_LEN - 1
    for j in range(B_J):
        o_ref[0, :, j * Q_BLOCK:(j + 1) * Q_BLOCK] = _bias_from_dist(t - cmp_end, rb_ref, g * B_J + j)


def _bias_win_kernel(rb_ref, o_ref):
    g, d = pl.program_id(0), pl.program_id(1)
    dist = d * Q_BLOCK + _iota((Q_BLOCK, Q_BLOCK), 1) - _iota((Q_BLOCK, Q_BLOCK), 0)
    for j in range(B_J):
        o_ref[0, 0, :, j * Q_BLOCK:(j + 1) * Q_BLOCK] = _bias_from_dist(dist, rb_ref, g * B_J + j)


def _bias_tables(rel_bias, T, n_pad):
    smem = pl.BlockSpec(memory_space=pltpu.SMEM)
    jq = B_J * Q_BLOCK
    n_q = T // Q_BLOCK
    bias_c = pl.pallas_call(
        _bias_cmp_kernel,
        grid=(B_GROUPS, n_q),
        in_specs=[smem],
        out_specs=pl.BlockSpec((1, n_pad, jq), lambda g, i: (g, 0, i)),
        out_shape=jax.ShapeDtypeStruct((B_GROUPS, n_pad, n_q * jq), F32),
        compiler_params=_cparams(("arbitrary", "arbitrary")),
        name="nsa_bias_cmp",
    )(rel_bias)
    bias_w = pl.pallas_call(
        _bias_win_kernel,
        grid=(B_GROUPS, N_WIN_TILES),
        in_specs=[smem],
        out_specs=pl.BlockSpec((1, 1, Q_BLOCK, jq), lambda g, d: (g, d, 0, 0)),
        out_shape=jax.ShapeDtypeStruct((B_GROUPS, N_WIN_TILES, Q_BLOCK, jq), F32),
        compiler_params=_cparams(("arbitrary", "arbitrary")),
        name="nsa_bias_win",
    )(rel_bias)
    return bias_c, bias_w


def _compress_kernel(x_ref, pe_ref, w1_ref, w2_ref, o_ref):
    n_grp = o_ref.shape[2]
    hidden = w1_ref.shape[2]
    ha = jnp.zeros((n_grp, hidden), F32)
    hb = jnp.zeros((n_grp, hidden), F32)
    for tau in range(CMP_STRIDE):
        x_tau = x_ref[0, pl.ds(tau, n_grp, stride=CMP_STRIDE), :]
        ha = ha + _mm(x_tau + pe_ref[tau:tau + 1, :], w1_ref[tau])
        hb = hb + _mm(x_tau + pe_ref[CMP_STRIDE + tau:CMP_STRIDE + tau + 1, :], w1_ref[CMP_STRIDE + tau])
    h = ha + pltpu.roll(hb, n_grp - 1, 0)
    out = _mm(_silu(h), w2_ref[...])
    o_ref[0, 0] = jnp.where(_iota(out.shape, 0) < n_grp - 1, out, 0.0)


def _compress(p, pe, w1, w2):
    Bsz, T, _ = p.shape
    n_grp = T // CMP_STRIDE
    return pl.pallas_call(
        _compress_kernel,
        grid=(Bsz, B_GROUPS),
        in_specs=[pl.BlockSpec((1, T, LANES), lambda b, g: (b, 0, COL_B + NSA_CMP + g)),
                  pl.BlockSpec(pe.shape, lambda b, g: (0, 0)),
                  pl.BlockSpec(w1.shape, lambda b, g: (0, 0, 0)),
                  pl.BlockSpec(w2.shape, lambda b, g: (0, 0))],
        out_specs=pl.BlockSpec((1, 1, n_grp, LANES), lambda b, g: (b, g, 0, 0)),
        out_shape=jax.ShapeDtypeStruct((Bsz, B_GROUPS, n_grp, LANES), F32),
        compiler_params=_cparams(("arbitrary", "arbitrary")),
        name="nsa_compress",
    )(p, pe, w1, w2)


def _nsa_kernel(q_ref, gate_ref, kvc_ref, ks_ref, kw_ref, bc_ref, bw_ref, cov_ref, o_ref,
                ks_scr, kw_scr, vs_scr, vw_scr, gate_scr, sc_scr, vs2_scr, qf_scr):
    QB, J, HD = Q_BLOCK, B_J, B_HD
    JQ = J * QB
    grp = pl.program_id(1)
    qb = pl.program_id(2)
    t0 = qb * QB
    n_pad = kvc_ref.shape[2]
    n_tiles = ks_ref.shape[1] // QB
    NEG = -jnp.inf
    MASKED = -1e30
    EMPTY = -1e29
    dot = functools.partial(jnp.dot, preferred_element_type=F32)

    @pl.when(qb == 0)
    def _():
        lane = _iota((1, LANES), 1)
        key_blk = _iota((QB, 1), 0) // SEL_LEN

        def fill(pair, carry):
            for half in range(2):
                kt = 2 * pair + half
                rows = pl.ds(pl.multiple_of(kt * QB, QB), QB)
                tile = ks_ref[0, rows, :]
                k_ones = jnp.where(lane < HD, tile, jnp.where(lane < HD + 2, 1.0, 0.0))
                one_hot = (lane == key_blk + kt * (QB // SEL_LEN)).astype(F32)
                ks_scr[rows, :] = jnp.concatenate([k_ones, one_hot], axis=1).astype(BF16)
                v_t = tile.T[HD:, :].astype(BF16)
                vs_scr[kt] = v_t
                vs2_scr[pair, :, half * QB:(half + 1) * QB] = v_t
                tile = kw_ref[0, rows, :]
                kw_scr[rows, :] = tile.astype(BF16)
                vw_scr[kt] = tile.T[HD:, :].astype(BF16)
            return carry
        lax.fori_loop(0, n_tiles // 2, fill, 0)

    q_rows = (q_ref[0] * (HD ** -0.5 * LOG2E)).T
    q_t = jnp.concatenate([q_rows[j * HD:(j + 1) * HD] for j in range(J)], axis=1)
    q_t = jnp.concatenate([q_t, jnp.zeros_like(q_t)], axis=0).astype(BF16)
    i_lane = _iota((1, JQ), 1) % QB
    t_lane = t0 + i_lane
    t_q = t0 + _iota((1, QB), 1)

    def finish(e_sum):
        return 1.0 / jnp.maximum(e_sum, 1e-30)

    def softmax_parts(scores):
        m = functools.reduce(jnp.maximum, [jnp.max(sc, axis=0, keepdims=True) for sc in scores])
        m = jnp.where(m == NEG, 0.0, m)
        es = [jnp.exp2(sc - m) for sc in scores]
        tot = functools.reduce(jnp.add, [jnp.sum(e, axis=0, keepdims=True) for e in es])
        return es, finish(tot)

    n_col = _iota((n_pad, 1), 0)
    mask_c = ((n_col * CMP_STRIDE + CMP_LEN - 1) <= t_lane) & (n_col < n_pad - 1)
    kvc = kvc_ref[0, 0].astype(BF16)
    s = jnp.where(mask_c, dot(kvc, q_t) + bc_ref[0], NEG)
    (e_c,), inv_c = softmax_parts([s])
    p_c = e_c * inv_c
    o_c = _mm_tn(kvc, p_c)[HD:]

    p_sum = p_c[:, 0:QB]
    for j in range(1, J):
        p_sum = p_sum + p_c[:, j * QB:(j + 1) * QB]
    ph = p_sum.astype(BF16)
    pl_ = (p_sum - ph.astype(F32)).astype(BF16)
    cov = cov_ref[...]
    imp = dot(cov, ph) + dot(cov, pl_)
    n_blk = cov.shape[0]
    m_col = _iota((n_blk, 1), 0)
    cur = t_q // SEL_LEN
    causal = m_col <= cur
    forced = (m_col == 0) | ((m_col > cur - N_LOCAL) & causal)
    score = jnp.where(causal, jnp.where(forced, jnp.inf, imp), NEG)
    n_real = ks_ref.shape[1] // SEL_LEN
    st = score[:n_real, :]
    blk = _iota((n_real, 1), 0)
    cnt = jnp.zeros((n_real, QB), F32)
    for mp in range(n_real):
        other = st[mp:mp + 1, :]
        beats = (other > st) | ((other == st) & (mp < blk))
        cnt = cnt + beats.astype(F32)
    selected = cnt < float(SEL_TOP)

    key_col = _iota((QB, 1), 0)
    q_rows = q_t[:HD].astype(F32)
    mask_rows = jnp.where(selected, 0.0, MASKED)
    if n_real < n_blk:
        mask_rows = jnp.concatenate([mask_rows, jnp.full((n_blk - n_real, QB), MASKED, F32)], axis=0)
    mask_rows = jnp.concatenate([mask_rows] * J, axis=1)
    far_bias = bw_ref[0, 2, 0:1, :]
    fb_hi = far_bias.astype(BF16).astype(F32)
    row = _iota((HD, 1), 0)
    bias_rows = jnp.where(row == 0, fb_hi, jnp.where(row == 1, far_bias - fb_hi, 0.0))
    qf_scr[0] = jnp.concatenate([q_rows, bias_rows, mask_rows], axis=0).astype(BF16)
    qf_scr[1] = jnp.concatenate([jnp.zeros((2 * HD, JQ), F32), jnp.full((n_blk, JQ), MASKED, F32)],
                                axis=0).astype(BF16)
    q_near = jnp.concatenate([q_rows, jnp.zeros_like(bias_rows), mask_rows], axis=0).astype(BF16)

    def key_tile(scr, kt):
        return scr[pl.ds(pl.multiple_of(kt * QB, QB), QB), :]

    def online(carry, scores, pv):
        m_old, l_old, acc = carry
        m_new = functools.reduce(jnp.maximum, [m_old] + [jnp.max(sc, axis=0, keepdims=True) for sc in scores])
        m_use = jnp.where(m_new < EMPTY, 0.0, m_new)
        alpha = jnp.exp2(m_old - m_use)
        prs = [jnp.exp2(sc - m_use) for sc in scores]
        l_new = alpha * l_old + functools.reduce(jnp.add, [jnp.sum(pr, axis=0, keepdims=True) for pr in prs])
        return m_new, l_new, alpha * acc + pv([pr.astype(BF16) for pr in prs])

    S = SEL_TILES_PER_STEP
    n_far = jnp.maximum(qb - 1, 0)

    def stage_scores(group, slot):
        for u in range(S):
            kt = group * S + u
            rhs = qf_scr[jnp.where(kt < n_far, 0, 1)]
            sc_scr[slot, u * QB:(u + 1) * QB, :] = dot(key_tile(ks_scr, jnp.minimum(kt, n_tiles - 1)), rhs)

    def far_step(it, carry):
        slot = it % 2
        for h in range(S // 2):
            v_pair = vs2_scr[jnp.minimum(it * (S // 2) + h, n_tiles // 2 - 1)]
            carry = online(carry, [sc_scr[slot, (2 * h + u) * QB:(2 * h + u + 1) * QB, :] for u in range(2)],
                           lambda prs, v_pair=v_pair: dot(v_pair, jnp.concatenate(prs, axis=0)))
        stage_scores(it + 1, 1 - slot)
        return carry

    state = (jnp.full((1, JQ), NEG, F32), jnp.zeros((1, JQ), F32), jnp.zeros((HD, JQ), F32))
    stage_scores(0, 0)
    state = lax.fori_loop(0, (n_far + S - 1) // S, far_step, state)
    kt_prev = jnp.maximum(qb - 1, 0)
    s_prev = jnp.where(qb >= 1, dot(key_tile(ks_scr, kt_prev), q_near) + bw_ref[0, 1], MASKED)
    s_diag = jnp.where(key_col <= i_lane, dot(key_tile(ks_scr, qb), q_near) + bw_ref[0, 0], MASKED)
    _, l_s, acc_s = online(state, [s_prev, s_diag],
                           lambda prs: dot(vs_scr[kt_prev], prs[0]) + dot(vs_scr[qb], prs[1]))
    o_s = acc_s * finish(l_s)

    scores, values = [], []
    for d in range(N_WIN_TILES):
        kt = jnp.maximum(qb - d, 0)
        start = pl.multiple_of(kt * QB, QB)
        sc = dot(kw_scr[pl.ds(start, QB), :], q_t) + bw_ref[0, d]
        ok = qb >= d
        if d == 0:
            ok = ok & (key_col <= i_lane)
        elif d == N_WIN_TILES - 1:
            ok = ok & (key_col > i_lane)
        else:
            ok = jnp.broadcast_to(ok, (QB, JQ))
        scores.append(jnp.where(ok, sc, NEG))
        values.append(vw_scr[kt])
    es, inv_w = softmax_parts(scores)
    acc_w = functools.reduce(jnp.add, [dot(v_t, e.astype(BF16)) for v_t, e in zip(values, es)])
    o_w = acc_w * inv_w

    gate_scr[...] = _sigmoid(gate_ref[0]).T
    g_rows = gate_scr[pl.ds(pl.multiple_of(grp * NSA_GATE_STRIDE, NSA_GATE_STRIDE), NSA_GATE_STRIDE), :]
    g_c, g_s, g_w = (jnp.concatenate([g_rows[r * J + j:r * J + j + 1, :] for j in range(J)], axis=1)
                     for r in range(3))
    o_t = g_c * o_c + g_s * o_s + g_w * o_w
    o_ref[0] = jnp.concatenate([o_t[:, j * QB:(j + 1) * QB] for j in range(J)], axis=0).T


def _nsa(p, kvc, bias_c, bias_w, cover_t):
    Bsz, T, _ = p.shape
    G, J, HD, QB = B_GROUPS, B_J, B_HD, Q_BLOCK
    n_q = T // QB
    n_pad = kvc.shape[2]
    JQ = J * QB
    q_blk = COL_B * LANES // (J * HD)

    def kv_rows(first):
        return pl.BlockSpec((1, T, LANES), lambda b, g, i: (b, 0, COL_B + first + g))

    return pl.pallas_call(
        _nsa_kernel,
        grid=(Bsz, G, n_q),
        in_specs=[pl.BlockSpec((1, QB, J * HD), lambda b, g, i: (b, i, q_blk + g)),
                  pl.BlockSpec((1, QB, LANES), lambda b, g, i: (b, i, COL_B + NSA_GATE)),
                  pl.BlockSpec((1, 1, n_pad, LANES), lambda b, g, i: (b, g, 0, 0)),
                  kv_rows(NSA_SEL), kv_rows(NSA_WIN),
                  pl.BlockSpec((1, n_pad, JQ), lambda b, g, i: (g, 0, i)),
                  pl.BlockSpec((1, N_WIN_TILES, QB, JQ), lambda b, g, i: (g, 0, 0, 0)),
                  pl.BlockSpec(cover_t.shape, lambda b, g, i: (0, 0))],
        out_specs=pl.BlockSpec((1, QB, J * HD), lambda b, g, i: (b, i, g)),
        out_shape=jax.ShapeDtypeStruct((Bsz, T, G * J * HD), F32),
        scratch_shapes=[pltpu.VMEM((T, 2 * LANES), BF16), pltpu.VMEM((T, LANES), BF16),
                        pltpu.VMEM((n_q, HD, QB), BF16), pltpu.VMEM((n_q, HD, QB), BF16),
                        pltpu.VMEM((LANES, QB), F32),
                        pltpu.VMEM((2, SEL_TILES_PER_STEP * QB, JQ), F32),
                        pltpu.VMEM((n_q // 2, HD, 2 * QB), BF16),
                        pltpu.VMEM((2, 2 * HD + LANES, JQ), BF16)],
        compiler_params=_cparams(("arbitrary", "arbitrary", "arbitrary")),
        name="nsa_attention",
    )(p, p, kvc, p, p, bias_c, bias_w, cover_t)


def _merge_kernel(ya_ref, yb_ref, yc_ref, ga_ref, gb_ref, gc_ref, x_ref, gt_ref, wb_ref, wo_ref, o_ref):
    merged = (_sigmoid(ga_ref[0]) * _mm(ya_ref[0], wb_ref[0])
              + _sigmoid(gb_ref[0]) * _mm(yb_ref[0], wb_ref[1])
              + _sigmoid(gc_ref[0]) * _mm(yc_ref[0], wb_ref[2]))
    o_ref[0] = x_ref[0] + gt_ref[0] * _mm(merged, wo_ref[...])


def _merge(ya, yb, yc, p, x, gt, wb, wo):
    Bsz, T, D = x.shape
    tm = 512
    bw = ya.shape[2]
    gate_blk = COL_GATE * LANES // D
    yspec = pl.BlockSpec((1, tm, bw), lambda b, i: (b, i, 0))

    def gspec(k):
        return pl.BlockSpec((1, tm, D), lambda b, i: (b, i, gate_blk + k))

    return pl.pallas_call(
        _merge_kernel,
        grid=(Bsz, T // tm),
        in_specs=[yspec, yspec, yspec, gspec(0), gspec(1), gspec(2),
                  pl.BlockSpec((1, tm, D), lambda b, i: (b, i, 0)),
                  pl.BlockSpec((1, 1, D), lambda b, i: (b, 0, 0)),
                  pl.BlockSpec(wb.shape, lambda b, i: (0, 0, 0)),
                  pl.BlockSpec(wo.shape, lambda b, i: (0, 0))],
        out_specs=pl.BlockSpec((1, tm, D), lambda b, i: (b, i, 0)),
        out_shape=jax.ShapeDtypeStruct(x.shape, F32),
        compiler_params=_cparams(("arbitrary", "arbitrary")),
        name="merge_out_proj",
    )(ya, yb, yc, p, p, p, x, gt, wb, wo)


def _ffn_kernel(final, x_ref, nw_ref, sh_ref, sc_ref, gt_ref, w1_ref, w3_ref, w2_ref, fw_ref, o_ref,
                h_scr, acc_scr):
    f = pl.program_id(2)

    @pl.when(f == 0)
    def _():
        h_scr[...] = _norm_mod(x_ref[0], nw_ref[...], sh_ref[0], sc_ref[0]).astype(BF16)
        acc_scr[...] = jnp.zeros_like(acc_scr)

    h = h_scr[...]
    u = _silu(jnp.dot(h, w1_ref[...], preferred_element_type=F32)) * jnp.dot(h, w3_ref[...],
                                                                           preferred_element_type=F32)
    acc_scr[...] += _mm(u, w2_ref[...])

    @pl.when(f == pl.num_programs(2) - 1)
    def _():
        x = x_ref[0] + gt_ref[0] * acc_scr[...]
        if final:
            x = x * lax.rsqrt(jnp.mean(x * x, axis=-1, keepdims=True) + 1e-6) * fw_ref[...]
        o_ref[0] = x


def _ffn(x, nw, sh, sc, gt, w1, w3, w2, fw, final):
    Bsz, T, D = x.shape
    tm, tf = min(T, 1024), 256
    n_ff = w1.shape[1]
    vec = pl.BlockSpec((1, D), lambda b, i, f: (0, 0))
    bvec = pl.BlockSpec((1, 1, D), lambda b, i, f: (b, 0, 0))
    xspec = pl.BlockSpec((1, tm, D), lambda b, i, f: (b, i, 0))
    return pl.pallas_call(
        functools.partial(_ffn_kernel, final),
        grid=(Bsz, T // tm, n_ff // tf),
        in_specs=[xspec, vec, bvec, bvec, bvec,
                  pl.BlockSpec((D, tf), lambda b, i, f: (0, f)),
                  pl.BlockSpec((D, tf), lambda b, i, f: (0, f)),
                  pl.BlockSpec((tf, D), lambda b, i, f: (f, 0)),
                  vec],
        out_specs=xspec,
        out_shape=jax.ShapeDtypeStruct(x.shape, F32),
        scratch_shapes=[pltpu.VMEM((tm, D), BF16), pltpu.VMEM((tm, D), F32)],
        compiler_params=_cparams(("arbitrary", "arbitrary", "arbitrary")),
        name="ffn_swiglu",
    )(x, nw, sh, sc, gt, w1, w3, w2, fw)


def _pack_w_in(w):
    a_in, b_in, c_in = 2048, 1304, 1792
    wa = w[:, :a_in]
    wb = w[:, a_in:a_in + b_in]
    wc = w[:, a_in + b_in:a_in + b_in + c_in]
    wg = w[:, a_in + b_in + c_in:]
    src, used = _nsa_columns()
    wb = jnp.where(used[None, :], jnp.take(wb, src, axis=1), 0.0)
    return jnp.concatenate([wa, wg, wc, wb], axis=1).astype(BF16)


def _nsa_columns():
    G, J, HD = B_GROUPS, B_J, B_HD
    width = NSA_BLOCKS * LANES
    src = np.zeros((width,), np.int32)
    used = np.zeros((width,), bool)
    n_q = G * J * HD
    src[:n_q] = np.arange(n_q)
    used[:n_q] = True
    for pair, blk in enumerate((NSA_CMP, NSA_SEL, NSA_WIN)):
        for g in range(G):
            for kv in range(2):
                dst = (blk + g) * LANES + kv * HD
                src[dst:dst + HD] = n_q + (2 * pair + kv) * G * HD + g * HD + np.arange(HD)
                used[dst:dst + HD] = True
    gate0 = n_q + 6 * G * HD
    for g in range(G):
        for j in range(J):
            for r in range(3):
                dst = NSA_GATE * LANES + g * NSA_GATE_STRIDE + r * J + j
                src[dst] = gate0 + (g * J + j) * 3 + r
                used[dst] = True
    return src, used


def _nsa_layer(p, pe_k, w1_k, w2_k, pe_v, w1_v, w2_v, bias_c, bias_w, cover):
    HD = B_HD
    w1k = w1_k.reshape(CMP_LEN, HD, CMP_HIDDEN)
    w1v = w1_v.reshape(CMP_LEN, HD, CMP_HIDDEN)
    zero1 = jnp.zeros_like(w1k)
    w1 = jnp.concatenate([jnp.concatenate([w1k, zero1], axis=2),
                          jnp.concatenate([zero1, w1v], axis=2)], axis=1).astype(BF16)
    zero2 = jnp.zeros_like(w2_k)
    w2 = jnp.concatenate([jnp.concatenate([w2_k, zero2], axis=1),
                          jnp.concatenate([zero2, w2_v], axis=1)], axis=0).astype(BF16)
    pe = jnp.concatenate([pe_k, pe_v], axis=1)
    kvc = _compress(p, pe, w1, w2)
    return _nsa(p, kvc, bias_c, bias_w, cover)


def _cover_matrix(T, n_pad):
    n = jnp.arange(n_pad)[None, :] * CMP_STRIDE
    m = jnp.arange(LANES)[:, None] * SEL_LEN
    real = (jnp.arange(n_pad)[None, :] < (T - CMP_LEN) // CMP_STRIDE + 1) & (jnp.arange(LANES)[:, None] < T // SEL_LEN)
    return ((n < m + SEL_LEN) & (n + CMP_LEN > m) & real).astype(BF16)


def kernel(x, c, ada_w, ada_b, norm1_w, norm2_w, w_in, hgrn_lb_logits, hgrn_norm_w, nsa_pe_k, nsa_cmp_w1_k, nsa_cmp_w2_k, nsa_pe_v, nsa_cmp_w1_v, nsa_cmp_w2_v, rel_bias, rw_mu, rw_w0, rw_w2, rw_a0, rw_a2, rw_g2, rw_k_k, rw_k_a, rw_r_k, rw_lnx_w, rw_lnx_b, w_branch, w_out, ffn_w1, ffn_w3, ffn_w2, final_norm_w):
    Bsz, T, D = x.shape
    L = ada_w.shape[0]
    n_pad = T // CMP_STRIDE
    c_pad = jnp.pad(c, ((0, 8 - Bsz), (0, 0)))
    ada = _ada(c_pad, ada_w, ada_b)[:, :Bsz]
    bias_c, bias_w = _bias_tables(rel_bias, T, n_pad)
    cover = _cover_matrix(T, n_pad)
    zeros64 = jnp.zeros((C_DECAY_LORA, C_HEADS * C_HD), F32)
    for l in range(L):
        sh1, sc1, gt1, sh2, sc2, gt2 = (ada[l, :, None, i * D:(i + 1) * D] for i in range(6))
        p = _inproj(x, norm1_w[l].reshape(1, D), sh1, sc1, _pack_w_in(w_in[l]))
        y_a = _hgrn(p, hgrn_lb_logits, hgrn_norm_w[l], l)
        y_b = _nsa_layer(p, nsa_pe_k[l], nsa_cmp_w1_k[l], nsa_cmp_w2_k[l],
                         nsa_pe_v[l], nsa_cmp_w1_v[l], nsa_cmp_w2_v[l], bias_c, bias_w, cover)
        y_c = _rwkv(p, rw_mu[l],
                    jnp.concatenate([rw_w2[l], zeros64], axis=0),
                    jnp.concatenate([zeros64, rw_a2[l]], axis=0),
                    rw_g2[l], rw_w0[l], rw_a0[l], rw_k_k[l], rw_k_a[l], rw_r_k[l], rw_lnx_w[l], rw_lnx_b[l])
        x = _merge(y_a, y_b, y_c, p, x, gt1, w_branch[l].astype(BF16), w_out[l].astype(BF16))
        x = _ffn(x, norm2_w[l].reshape(1, D), sh2, sc2, gt2,
                 ffn_w1[l].astype(BF16), ffn_w3[l].astype(BF16), ffn_w2[l].astype(BF16),
                 final_norm_w.reshape(1, D), l == L - 1)
    return x
```

```python
import functools
import math

import jax
import jax.numpy as jnp
import numpy as np
from jax import lax
from jax.experimental import pallas as pl
from jax.experimental.pallas import tpu as pltpu

F32 = jnp.float32
BF16 = jnp.bfloat16
I32 = jnp.int32

LANES = 128
VMEM_LIMIT = 56 * 1024 * 1024

D_MODEL = 1024
DEPTH = 4
A_HEADS, A_DK, A_DV = 4, 128, 128
A_CHUNK, A_SUB = 64, 16
B_HEADS, B_GROUPS, B_HD = 8, 2, 64
B_J = B_HEADS // B_GROUPS
CMP_LEN, CMP_STRIDE, CMP_HIDDEN = 32, 16, 256
SEL_LEN, SEL_TOP, N_LOCAL = 64, 16, 2
WINDOW, Q_BLOCK = 512, 128
N_WIN_TILES = WINDOW // Q_BLOCK + 1
SEL_TILES_PER_STEP = 4
REL_BUCKETS, REL_MAX_DIST = 32, 128
LOG2E = math.log2(math.e)
NSA_MASKED = -1e30
NSA_EMPTY = -1e29
C_HEADS, C_HD = 8, 64
C_CHUNK = 64
C_SUBS = 4
C_PACK = 4
C_DECAY_LORA, C_AAA_LORA, C_GATE_LORA = 64, 64, 128
LNX_EPS = 64e-5
D_FF = 2816

COL_A = 0
COL_GATE = 2048 // LANES
COL_C = 5120 // LANES
COL_B = 6912 // LANES
P_COLS = 8448
P_DTYPE = BF16
NSA_Q, NSA_CMP, NSA_SEL, NSA_WIN, NSA_GATE, NSA_BLOCKS = 0, 4, 6, 8, 10, 12
NSA_GATE_STRIDE = 16


def _cparams(sem):
    return pltpu.CompilerParams(dimension_semantics=sem, vmem_limit_bytes=VMEM_LIMIT)


def _split3(x):
    hi = x.astype(BF16)
    r1 = x - hi.astype(F32)
    mid = r1.astype(BF16)
    lo = (r1 - mid.astype(F32)).astype(BF16)
    return hi, mid, lo


def _mm(a, b):
    return jnp.dot(a.astype(BF16), b.astype(BF16), preferred_element_type=F32)


def _mm_nt(a, b):
    return lax.dot_general(a.astype(BF16), b.astype(BF16), (((1,), (1,)), ((), ())),
                           preferred_element_type=F32)


def _mm_tn(a, b):
    return lax.dot_general(a.astype(BF16), b.astype(BF16), (((0,), (0,)), ((), ())),
                           preferred_element_type=F32)


def _mm_exact_lhs(a_exact, b):
    a = a_exact.astype(BF16)
    hi, mid, lo = _split3(b)
    d = functools.partial(jnp.dot, preferred_element_type=F32)
    return d(a, hi) + d(a, mid) + d(a, lo)


def _mm_exact_rhs(a, b_exact):
    b = b_exact.astype(BF16)
    hi, mid, lo = _split3(a)
    d = functools.partial(jnp.dot, preferred_element_type=F32)
    return d(hi, b) + d(mid, b) + d(lo, b)


def _mm_x3(a, b):
    ah = a.astype(BF16)
    al = (a - ah.astype(F32)).astype(BF16)
    bh = b.astype(BF16)
    bl = (b - bh.astype(F32)).astype(BF16)
    d = functools.partial(jnp.dot, preferred_element_type=F32)
    return d(ah, bh) + d(ah, bl) + d(al, bh)


def _sigmoid(x):
    return 1.0 / (1.0 + jnp.exp(-x))


def _silu(x):
    return x * _sigmoid(x)


def _log_sigmoid(x):
    return jnp.minimum(x, 0.0) - jnp.log(1.0 + jnp.exp(-jnp.abs(x)))


def _iota(shape, dim):
    return lax.broadcasted_iota(I32, shape, dim)


def _ada_kernel(c_ref, w_ref, b_ref, o_ref):
    c = c_ref[...]
    o_ref[0] = _mm_x3(_silu(c), w_ref[0]) + b_ref[0]


def _ada(c_pad, ada_w, ada_b):
    L = ada_w.shape[0]
    n_out = ada_w.shape[2]
    tn = D_MODEL
    return pl.pallas_call(
        _ada_kernel,
        grid=(L, n_out // tn),
        in_specs=[pl.BlockSpec((8, D_MODEL), lambda l, j: (0, 0)),
                  pl.BlockSpec((1, D_MODEL, tn), lambda l, j: (l, 0, j)),
                  pl.BlockSpec((1, 1, tn), lambda l, j: (l, 0, j))],
        out_specs=pl.BlockSpec((1, 8, tn), lambda l, j: (l, 0, j)),
        out_shape=jax.ShapeDtypeStruct((L, 8, n_out), F32),
        compiler_params=_cparams(("arbitrary", "arbitrary")),
        name="ada_ln",
    )(c_pad, ada_w, ada_b.reshape(L, 1, n_out))


def _norm_mod(x, nw, sh, sc):
    y = x * lax.rsqrt(jnp.mean(x * x, axis=-1, keepdims=True) + 1e-6) * nw
    return y * (1.0 + sc) + sh


def _inproj_kernel(x_ref, nw_ref, sh_ref, sc_ref, w_ref, o_ref, h_scr):
    @pl.when(pl.program_id(2) == 0)
    def _():
        h_scr[...] = _norm_mod(x_ref[0], nw_ref[...], sh_ref[0], sc_ref[0]).astype(BF16)

    o_ref[0] = jnp.dot(h_scr[...], w_ref[...], preferred_element_type=F32).astype(o_ref.dtype)


def _inproj(x, nw, sh, sc, w):
    Bsz, T, D = x.shape
    n_out = w.shape[1]
    tm, tn = min(T, 1024), 768
    return pl.pallas_call(
        _inproj_kernel,
        grid=(Bsz, T // tm, n_out // tn),
        in_specs=[pl.BlockSpec((1, tm, D), lambda b, i, n: (b, i, 0)),
                  pl.BlockSpec((1, D), lambda b, i, n: (0, 0)),
                  pl.BlockSpec((1, 1, D), lambda b, i, n: (b, 0, 0)),
                  pl.BlockSpec((1, 1, D), lambda b, i, n: (b, 0, 0)),
                  pl.BlockSpec((D, tn), lambda b, i, n: (0, n))],
        out_specs=pl.BlockSpec((1, tm, tn), lambda b, i, n: (b, i, n)),
        out_shape=jax.ShapeDtypeStruct((Bsz, T, n_out), P_DTYPE),
        scratch_shapes=[pltpu.VMEM((tm, D), BF16)],
        compiler_params=_cparams(("arbitrary", "arbitrary", "arbitrary")),
        name="in_proj",
    )(x, nw, sh, sc, w)


def _hgrn_kernel(layer, lbl_ref, nw_ref, q_ref, f_ref, i_ref, g_ref, o_ref, s_scr):
    C, c = A_CHUNK, A_SUB
    slabs = [slice(h * A_DK, (h + 1) * A_DK) for h in range(A_HEADS)]

    @pl.when(pl.program_id(1) == 0)
    def _():
        s_scr[...] = jnp.zeros_like(s_scr)

    logits = lbl_ref[...]
    ex = jnp.exp(logits - jnp.max(logits, axis=0, keepdims=True))
    sm = ex / jnp.sum(ex, axis=0, keepdims=True)
    lb = jnp.zeros((1, logits.shape[1]), F32)
    for i in range(1, layer + 1):
        lb = lb + sm[i:i + 1, :]

    q = _silu(q_ref[0].astype(F32))
    fr = f_ref[0].astype(F32)
    v = i_ref[0].astype(F32)
    la = jnp.log(lb)
    lc = jnp.log(1.0 - lb) + _log_sigmoid(fr)
    log_f = jnp.maximum(la, lc) + jnp.log(1.0 + jnp.exp(-jnp.abs(la - lc)))
    k = (1.0 - lb) * _sigmoid(-fr)

    tri = (_iota((C, C), 0) >= _iota((C, C), 1)).astype(BF16)
    b = _mm_exact_lhs(tri, log_f) * LOG2E
    b_last = b[C - 1:C, :]

    qe = q * jnp.exp2(b)
    o = [_mm_nt(qe[:, sl], s_scr[h]) for h, sl in enumerate(slabs)]
    off = [[jnp.zeros((c, A_DV), F32)] for _ in slabs]
    for blk in range(1, C // c):
        lo = blk * c
        ref = b[lo - 1:lo, :]
        qi = q[lo:lo + c] * jnp.exp2(b[lo:lo + c] - ref)
        ki = k[:lo] * jnp.exp2(ref - b[:lo])
        att = [_mm_nt(qi[:, sl], ki[:, sl]) for sl in slabs]
        for h, sl in enumerate(slabs):
            off[h].append(_mm(att[h], v[:lo, sl]))
    o = [o[h] + jnp.concatenate(off[h], axis=0) for h in range(len(slabs))]
    row = _iota((C, 1), 0) % c
    for j in range(c):
        if j == 0:
            kj, bj, vj = k, b, v
        else:
            kj, bj, vj = (pltpu.roll(t, j, 0) for t in (k, b, v))
        ok = row >= j
        term = jnp.where(ok, q * kj * jnp.exp2(jnp.minimum(b - bj, 0.0)), 0.0)
        for h, sl in enumerate(slabs):
            o[h] = o[h] + jnp.sum(term[:, sl], axis=1, keepdims=True) * vj[:, sl]

    kd = k * jnp.exp2(b_last - b)
    p_end = jnp.exp2(b_last)
    for h, sl in enumerate(slabs):
        s_scr[h] = s_scr[h] * p_end[:, sl] + _mm_tn(v[:, sl], kd[:, sl])

    gate = _sigmoid(g_ref[0].astype(F32))
    for h, sl in enumerate(slabs):
        oh = o[h]
        oh = oh * lax.rsqrt(jnp.mean(oh * oh, axis=-1, keepdims=True) + 1e-5) * nw_ref[:, sl]
        o_ref[0, :, sl] = oh * gate[:, sl]


def _hgrn(p, lb_logits, norm_w, layer):
    Bsz, T, _ = p.shape
    C = A_CHUNK
    L = lb_logits.shape[0]
    W = A_HEADS * A_DK
    cw = COL_A * LANES // W

    def col(i):
        return pl.BlockSpec((1, C, W), lambda b, c: (b, c, cw + i))

    return pl.pallas_call(
        functools.partial(_hgrn_kernel, layer),
        grid=(Bsz, T // C),
        in_specs=[pl.BlockSpec((L, W), lambda b, c: (0, 0)),
                  pl.BlockSpec((1, A_HEADS * A_DV), lambda b, c: (0, 0)),
                  col(0), col(1), col(2), col(3)],
        out_specs=pl.BlockSpec((1, C, A_HEADS * A_DV), lambda b, c: (b, c, 0)),
        out_shape=jax.ShapeDtypeStruct((Bsz, T, A_HEADS * A_DV), F32),
        scratch_shapes=[pltpu.VMEM((A_HEADS, A_DV, A_DK), F32)],
        compiler_params=_cparams(("arbitrary", "arbitrary")),
        name="hgrn2",
    )(lb_logits, norm_w.reshape(1, -1), p, p, p, p)


def _rwkv_kernel(r_ref, k_ref, v_ref, wa_ref, xg_ref, mu_ref,
                 w2_ref, a2_ref, g2_ref, w0_ref, a0_ref, kk_ref, ka_ref, rk_ref, lnw_ref, lnb_ref,
                 o_ref, s_scr, pr_scr, pk_scr, pv_scr, pwa_scr, pg_scr):
    C = C_CHUNK
    W = C_HEADS * C_HD
    n_hp = W // LANES

    @pl.when(pl.program_id(1) == 0)
    def _():
        s_scr[...] = jnp.zeros_like(s_scr)
        for scr in (pr_scr, pk_scr, pv_scr, pwa_scr, pg_scr):
            scr[...] = jnp.zeros_like(scr)

    R = C_SUBS * C
    subs = [slice(s * C, (s + 1) * C) for s in range(C_SUBS)]
    row0 = _iota((R, 1), 0) == 0

    def shifted(ref, prev_scr, mu):
        cur = ref[0].astype(F32)
        prev = jnp.where(row0, prev_scr[...], pltpu.roll(cur, 1, 0))
        prev_scr[...] = cur[R - 1:R, :]
        return cur + (prev - cur) * mu

    r = shifted(r_ref, pr_scr, mu_ref[:, 0:W])
    k = shifted(k_ref, pk_scr, mu_ref[:, W:2 * W])
    v = shifted(v_ref, pv_scr, mu_ref[:, 2 * W:3 * W])
    xwa = shifted(wa_ref, pwa_scr, mu_ref[:, 3 * W:3 * W + LANES])
    xg = shifted(xg_ref, pg_scr, mu_ref[:, 3 * W + LANES:])

    w = w0_ref[...] + _mm_x3(jnp.tanh(xwa), w2_ref[...])
    sp = jnp.maximum(-w, 0.0) + jnp.log(1.0 + jnp.exp(-jnp.abs(w)))
    lw = -jnp.exp(-sp - 0.5)
    a = _sigmoid(a0_ref[...] + _mm_x3(xwa, a2_ref[...]))
    g = _mm_x3(_sigmoid(xg), g2_ref[...])

    ri, rj = _iota((R, R), 0), _iota((R, R), 1)
    tri_incl = ((ri >= rj) & (ri // C == rj // C)).astype(BF16)
    b = _mm_exact_lhs(tri_incl, lw)
    b_last = [b[rs.stop - 1:rs.stop, :] for rs in subs]
    p_end = [jnp.exp(bl) for bl in b_last]
    inv_p = jnp.exp(-b)
    e_end = jnp.concatenate([jnp.exp(bl - b[rs]) for bl, rs in zip(b_last, subs)], axis=0)
    kk_raw = k * kk_ref[...]
    k = k * (1.0 + (a - 1.0) * ka_ref[...])
    bonus_in = r * k * rk_ref[...]
    r_t = r * jnp.exp(b)
    p_prev = jnp.exp(b - lw)

    bd_f = ((_iota((LANES, LANES), 0) < C_HD) == (_iota((LANES, LANES), 1) < C_HD)).astype(F32)
    slabs = [slice(hp * LANES, (hp + 1) * LANES) for hp in range(n_hp)]
    dot = functools.partial(jnp.dot, preferred_element_type=F32)
    bd_b = bd_f.astype(BF16)

    PH = C_PACK
    PW = PH * C_HD
    n_packs = W // PW
    packs = [slice(i * PW, (i + 1) * PW) for i in range(n_packs)]
    s_lane = _iota((C, PH * C), 1) % C
    t_row = _iota((C, PH * C), 0)
    lower_strict = t_row > s_lane
    lower_incl = t_row >= s_lane
    same_head = (_iota((PH * C, PW), 0) // C) == (_iota((PH * C, PW), 1) // C_HD)
    state_mask = (_iota((PW, PW), 0) // C_HD) == (_iota((PW, PW), 1) // C_HD)

    def spread(x):
        xb = x.astype(BF16)
        return jnp.where(same_head, jnp.concatenate([xb] * PH, axis=0), jnp.zeros((), BF16))

    def group_sums(x):
        st = jnp.concatenate([x[:, sl] for sl in slabs], axis=0)
        hi = st.astype(BF16)
        lo = (st - hi.astype(F32)).astype(BF16)
        out = dot(hi, bd_b) + dot(lo, bd_b)
        return jnp.concatenate([out[i * R:(i + 1) * R] for i in range(n_hp)], axis=1)

    kk = kk_raw * lax.rsqrt(jnp.maximum(group_sums(kk_raw * kk_raw), 1e-24))
    al = a * kk
    bonus = group_sums(bonus_in) * v
    kap = kk * p_prev
    al_h = al * inv_p
    k_h = k * inv_p
    al_e = al * e_end
    k_e = k * e_end

    PC = PH * C
    units = [(rs, pk) for rs in subs for pk in packs]
    lhs = [jnp.concatenate([kap[rs, pk], r_t[rs, pk]], axis=0).astype(BF16) for rs, pk in units]
    e_st = [jnp.concatenate([al_e[rs, pk], k_e[rs, pk]], axis=0).astype(BF16) for rs, pk in units]
    v_b = [v[rs, pk].astype(BF16) for rs, pk in units]
    v_sp = [spread(v[rs, pk]) for rs, pk in units]
    big = [_mm_nt(lhs[i], jnp.concatenate([spread(al_h[rs, pk]), spread(k_h[rs, pk])], axis=0))
           for i, (rs, pk) in enumerate(units)]
    a_ak = [jnp.where(lower_strict, m[:C, :PC], 0.0) for m in big]
    b_al = [jnp.where(lower_incl, m[C:, :PC], 0.0) for m in big]
    kv = [dot(jnp.concatenate([jnp.where(lower_strict, m[:C, PC:], 0.0),
                               jnp.where(lower_incl, m[C:, PC:], 0.0)], axis=0).astype(BF16), v_sp[i])
          for i, m in enumerate(big)]

    eye = (t_row == s_lane).astype(F32)
    n = [-m for m in a_ak]
    x = [eye + m for m in n]
    n = [dot(m.astype(BF16), spread(m)) for m in n]
    power = 2
    while 2 * power < C:
        for i in range(len(units)):
            res = dot(jnp.concatenate([x[i], n[i]], axis=0).astype(BF16), spread(n[i]))
            x[i] = x[i] + res[:C]
            n[i] = res[C:]
        power *= 2
    x = [x[i] + dot(x[i].astype(BF16), spread(n[i])) for i in range(len(units))]

    state = [s_scr[i] for i in range(n_packs)]
    y_rows = []
    for si in range(C_SUBS):
        y_parts = []
        for pi, pk in enumerate(packs):
            i = si * n_packs + pi
            ks = _mm_nt(lhs[i], state[pi])
            u = -dot(x[i].astype(BF16), spread(ks[:C] + kv[i][:C]))
            y_parts.append(ks[C:] + dot(b_al[i].astype(BF16), spread(u)) + kv[i][C:])
            upd = _mm_tn(jnp.concatenate([u.astype(BF16), v_b[i]], axis=0), e_st[i])
            state[pi] = state[pi] * p_end[si][:, pk] + jnp.where(state_mask, upd, 0.0)
        y_rows.append(jnp.concatenate(y_parts, axis=1))
    for pi in range(n_packs):
        s_scr[pi] = state[pi]

    y = jnp.concatenate(y_rows, axis=0)
    inv_hd = 1.0 / C_HD
    yc = y - group_sums(y) * inv_hd
    var = group_sums(yc * yc) * inv_hd
    y = yc * lax.rsqrt(var + LNX_EPS) * lnw_ref[...] + lnb_ref[...]
    o_ref[0] = (y + bonus) * g


def _rwkv(p, mu, w2p, a2p, g2, w0, a0, k_k, k_a, r_k, lnx_w, lnx_b):
    Bsz, T, _ = p.shape
    C = C_SUBS * C_CHUNK
    W = C_HEADS * C_HD
    n_hp = W // LANES
    cw = COL_C * LANES // W
    pw = C_PACK * C_HD
    assert C_CHUNK == C_HD and W % pw == 0

    def wide(i):
        return pl.BlockSpec((1, C, W), lambda b, c: (b, c, cw + i))

    def narrow(i):
        return pl.BlockSpec((1, C, LANES), lambda b, c: (b, c, COL_C + 3 * n_hp + i))

    def const(shape):
        return pl.BlockSpec(shape, lambda b, c: (0,) * len(shape))

    row = lambda t: t.reshape(1, -1)
    vecs = [row(t) for t in (w0, a0, k_k, k_a, r_k, lnx_w, lnx_b)]
    return pl.pallas_call(
        _rwkv_kernel,
        grid=(Bsz, T // C),
        in_specs=[wide(0), wide(1), wide(2), narrow(0), narrow(1), const((1, mu.shape[0])),
                  const(w2p.shape), const(a2p.shape), const(g2.shape)] + [const((1, W))] * 7,
        out_specs=pl.BlockSpec((1, C, W), lambda b, c: (b, c, 0)),
        out_shape=jax.ShapeDtypeStruct((Bsz, T, W), F32),
        scratch_shapes=[pltpu.VMEM((W // pw, pw, pw), F32)] + [pltpu.VMEM((1, W), F32)] * 3
                       + [pltpu.VMEM((1, LANES), F32)] * 2,
        compiler_params=_cparams(("arbitrary", "arbitrary")),
        name="rwkv7",
    )(p, p, p, p, p, row(mu), w2p, a2p, g2, *vecs)


def _t5_thresholds():
    max_exact = REL_BUCKETS // 2
    out = []
    for kk in range(1, REL_BUCKETS - max_exact):
        n = max_exact
        while int(math.log(n / max_exact) / math.log(REL_MAX_DIST / max_exact) * (REL_BUCKETS - max_exact)) < kk:
            n += 1
        out.append(n)
    return out


def _bias_from_dist(dist, rb_ref, head, valid):
    n = jnp.maximum(dist, 0)
    max_exact = REL_BUCKETS // 2
    large = jnp.full(n.shape, max_exact, I32)
    for thr in _t5_thresholds():
        large = large + (n >= thr).astype(I32)
    bucket = jnp.where(n < max_exact, n, large)
    out = jnp.zeros(n.shape, F32)
    for j in range(REL_BUCKETS):
        out = jnp.where(bucket == j, rb_ref[j, head] * LOG2E, out)
    return jnp.where(valid, out, NSA_MASKED)


def _bias_cmp_kernel(rb_ref, o_ref):
    g, qb = pl.program_id(0), pl.program_id(1)
    n_pad = o_ref.shape[1]
    t = qb * Q_BLOCK + _iota((n_pad, Q_BLOCK), 1)
    n = _iota((n_pad, Q_BLOCK), 0)
    dist = t - (n * CMP_STRIDE + CMP_LEN - 1)
    valid = (dist >= 0) & (n < n_pad - 1)
    for j in range(B_J):
        o_ref[0, :, j * Q_BLOCK:(j + 1) * Q_BLOCK] = _bias_from_dist(dist, rb_ref, g * B_J + j, valid)


def _bias_win_kernel(rb_ref, o_ref):
    g, d = pl.program_id(0), pl.program_id(1)
    dist = d * Q_BLOCK + _iota((Q_BLOCK, Q_BLOCK), 1) - _iota((Q_BLOCK, Q_BLOCK), 0)
    valid = (dist >= 0) & (dist < WINDOW) & (d < N_WIN_TILES)
    for j in range(B_J):
        o_ref[0, 0, :, j * Q_BLOCK:(j + 1) * Q_BLOCK] = _bias_from_dist(dist, rb_ref, g * B_J + j, valid)


def _bias_tables(rel_bias, T, n_pad):
    smem = pl.BlockSpec(memory_space=pltpu.SMEM)
    jq = B_J * Q_BLOCK
    n_q = T // Q_BLOCK
    bias_c = pl.pallas_call(
        _bias_cmp_kernel,
        grid=(B_GROUPS, n_q),
        in_specs=[smem],
        out_specs=pl.BlockSpec((1, n_pad, jq), lambda g, i: (g, 0, i)),
        out_shape=jax.ShapeDtypeStruct((B_GROUPS, n_pad, n_q * jq), F32),
        compiler_params=_cparams(("arbitrary", "arbitrary")),
        name="nsa_bias_cmp",
    )(rel_bias)
    bias_w = pl.pallas_call(
        _bias_win_kernel,
        grid=(B_GROUPS, N_WIN_TILES + 1),
        in_specs=[smem],
        out_specs=pl.BlockSpec((1, 1, Q_BLOCK, jq), lambda g, d: (g, d, 0, 0)),
        out_shape=jax.ShapeDtypeStruct((B_GROUPS, N_WIN_TILES + 1, Q_BLOCK, jq), F32),
        compiler_params=_cparams(("arbitrary", "arbitrary")),
        name="nsa_bias_win",
    )(rel_bias)
    return bias_c, bias_w


def _compress_kernel(x_ref, pe_ref, w1_ref, w2_ref, o_ref, xs_scr):
    n_grp = o_ref.shape[2]
    hidden = w1_ref.shape[2]
    ha = jnp.zeros((n_grp, hidden), F32)
    hb = jnp.zeros((n_grp, hidden), F32)
    xs_scr[...] = x_ref[0].astype(F32)
    for tau in range(CMP_STRIDE):
        x_tau = xs_scr[pl.ds(tau, n_grp, stride=CMP_STRIDE), :]
        ha = ha + _mm(x_tau + pe_ref[tau:tau + 1, :], w1_ref[tau])
        hb = hb + _mm(x_tau + pe_ref[CMP_STRIDE + tau:CMP_STRIDE + tau + 1, :], w1_ref[CMP_STRIDE + tau])
    h = ha + pltpu.roll(hb, n_grp - 1, 0)
    out = _mm(_silu(h), w2_ref[...])
    o_ref[0, 0] = jnp.where(_iota(out.shape, 0) < n_grp - 1, out, 0.0)


def _compress(p, pe, w1, w2):
    Bsz, T, _ = p.shape
    n_grp = T // CMP_STRIDE
    return pl.pallas_call(
        _compress_kernel,
        grid=(Bsz, B_GROUPS),
        in_specs=[pl.BlockSpec((1, T, LANES), lambda b, g: (b, 0, COL_B + NSA_CMP + g)),
                  pl.BlockSpec(pe.shape, lambda b, g: (0, 0)),
                  pl.BlockSpec(w1.shape, lambda b, g: (0, 0, 0)),
                  pl.BlockSpec(w2.shape, lambda b, g: (0, 0))],
        out_specs=pl.BlockSpec((1, 1, n_grp, LANES), lambda b, g: (b, g, 0, 0)),
        out_shape=jax.ShapeDtypeStruct((Bsz, B_GROUPS, n_grp, LANES), F32),
        scratch_shapes=[pltpu.VMEM((T, LANES), F32)],
        compiler_params=_cparams(("arbitrary", "arbitrary")),
        name="nsa_compress",
    )(p, pe, w1, w2)


def _nsa_kernel(q_ref, gate_ref, kvc_ref, ks_ref, kw_ref, bc_ref, bw_ref, cov_ref, o_ref,
                ks_scr, kw_scr, vs_scr, vw_scr, gate_scr, sc_scr, vs2_scr, qf_scr):
    QB, J, HD = Q_BLOCK, B_J, B_HD
    JQ = J * QB
    grp = pl.program_id(1)
    qb = pl.program_id(2)
    t0 = qb * QB
    n_pad = kvc_ref.shape[2]
    n_tiles = ks_ref.shape[1] // QB
    NEG = -jnp.inf
    MASKED, EMPTY = NSA_MASKED, NSA_EMPTY
    dot = functools.partial(jnp.dot, preferred_element_type=F32)

    @pl.when(qb == 0)
    def _():
        lane = _iota((1, LANES), 1)
        key_blk = _iota((QB, 1), 0) // SEL_LEN

        def fill(pair, carry):
            for half in range(2):
                kt = 2 * pair + half
                rows = pl.ds(pl.multiple_of(kt * QB, QB), QB)
                tile = ks_ref[0, rows, :].astype(F32)
                k_ones = jnp.where(lane < HD, tile, jnp.where(lane < HD + 2, 1.0, 0.0))
                one_hot = (lane == key_blk + kt * (QB // SEL_LEN)).astype(F32)
                ks_scr[rows, :] = jnp.concatenate([k_ones, one_hot], axis=1).astype(BF16)
                v_t = tile.T[HD:, :].astype(BF16)
                vs_scr[kt] = v_t
                vs2_scr[pair, :, half * QB:(half + 1) * QB] = v_t
                tile = kw_ref[0, rows, :].astype(F32)
                kw_scr[rows, :] = tile.astype(BF16)
                vw_scr[kt] = tile.T[HD:, :].astype(BF16)
            return carry
        lax.fori_loop(0, n_tiles // 2, fill, 0)

    q_rows = (q_ref[0].astype(F32) * (HD ** -0.5 * LOG2E)).T
    q_t = jnp.concatenate([q_rows[j * HD:(j + 1) * HD] for j in range(J)], axis=1)
    q_t = jnp.concatenate([q_t, jnp.zeros_like(q_t)], axis=0).astype(BF16)
    i_lane = _iota((1, JQ), 1) % QB
    t_lane = t0 + i_lane
    t_q = t0 + _iota((1, QB), 1)

    def finish(e_sum):
        return 1.0 / jnp.maximum(e_sum, 1e-30)

    def softmax_parts(scores):
        m = functools.reduce(jnp.maximum, [jnp.max(sc, axis=0, keepdims=True) for sc in scores])
        m = jnp.where(m < EMPTY, 0.0, m)
        es = [jnp.exp2(sc - m) for sc in scores]
        tot = functools.reduce(jnp.add, [jnp.sum(e, axis=0, keepdims=True) for e in es])
        return es, finish(tot)

    kvc = kvc_ref[0, 0].astype(BF16)
    (e_c,), inv_c = softmax_parts([dot(kvc, q_t) + bc_ref[0]])
    p_c = e_c * inv_c
    o_c = _mm_tn(kvc, p_c)[HD:]

    p_sum = p_c[:, 0:QB]
    for j in range(1, J):
        p_sum = p_sum + p_c[:, j * QB:(j + 1) * QB]
    ph = p_sum.astype(BF16)
    pl_ = (p_sum - ph.astype(F32)).astype(BF16)
    cov = cov_ref[...]
    imp = dot(cov, ph) + dot(cov, pl_)
    n_blk = cov.shape[0]
    m_col = _iota((n_blk, 1), 0)
    cur = t_q // SEL_LEN
    causal = m_col <= cur
    forced = (m_col == 0) | ((m_col > cur - N_LOCAL) & causal)
    score = jnp.where(causal, jnp.where(forced, jnp.inf, imp), NEG)
    n_real = ks_ref.shape[1] // SEL_LEN
    st = score[:n_real, :]
    SUB = 8
    row_in_group = _iota((SUB, 1), 0)
    groups = [st[lo:lo + SUB, :] for lo in range(0, n_real, SUB)]
    cnts = [jnp.zeros((SUB, QB), F32) for _ in groups]
    for mp in range(n_real):
        other = st[mp:mp + 1, :]
        for gi, rows in enumerate(groups):
            lo = gi * SUB
            if lo > mp:
                beats = other >= rows
            elif lo + SUB - 1 < mp:
                beats = other > rows
            else:
                beats = (other > rows) | ((other == rows) & (row_in_group > mp - lo))
            cnts[gi] = cnts[gi] + beats.astype(F32)
    selected = jnp.concatenate(cnts, axis=0) < float(SEL_TOP)

    key_col = _iota((QB, 1), 0)
    q_rows = q_t[:HD].astype(F32)
    mask_rows = jnp.where(selected, 0.0, MASKED)
    if n_real < n_blk:
        mask_rows = jnp.concatenate([mask_rows, jnp.full((n_blk - n_real, QB), MASKED, F32)], axis=0)
    mask_rows = jnp.concatenate([mask_rows] * J, axis=1)
    far_bias = bw_ref[0, 2, 0:1, :]
    fb_hi = far_bias.astype(BF16).astype(F32)
    row = _iota((HD, 1), 0)
    bias_rows = jnp.where(row == 0, fb_hi, jnp.where(row == 1, far_bias - fb_hi, 0.0))
    qf_scr[0] = jnp.concatenate([q_rows, bias_rows, mask_rows], axis=0).astype(BF16)
    qf_scr[1] = jnp.concatenate([jnp.zeros((2 * HD, JQ), F32), jnp.full((n_blk, JQ), MASKED, F32)],
                                axis=0).astype(BF16)
    q_near = jnp.concatenate([q_rows, jnp.zeros_like(bias_rows), mask_rows], axis=0).astype(BF16)

    def key_tile(scr, kt):
        return scr[pl.ds(pl.multiple_of(kt * QB, QB), QB), :]

    def online(carry, scores, pv):
        m_old, l_old, acc = carry
        m_new = functools.reduce(jnp.maximum, [m_old] + [jnp.max(sc, axis=0, keepdims=True) for sc in scores])
        m_use = jnp.where(m_new < EMPTY, 0.0, m_new)
        alpha = jnp.exp2(m_old - m_use)
        prs = [jnp.exp2(sc - m_use) for sc in scores]
        l_new = alpha * l_old + functools.reduce(jnp.add, [jnp.sum(pr, axis=0, keepdims=True) for pr in prs])
        return m_new, l_new, alpha * acc + pv([pr.astype(BF16) for pr in prs])

    S = SEL_TILES_PER_STEP
    n_far = jnp.maximum(qb - 1, 0)

    def stage_scores(group, slot):
        for u in range(S):
            kt = group * S + u
            rhs = qf_scr[jnp.where(kt < n_far, 0, 1)]
            sc_scr[slot, u * QB:(u + 1) * QB, :] = dot(key_tile(ks_scr, jnp.minimum(kt, n_tiles - 1)), rhs)

    def far_step(it, carry):
        slot = it % 2
        for h in range(S // 2):
            v_pair = vs2_scr[jnp.minimum(it * (S // 2) + h, n_tiles // 2 - 1)]
            carry = online(carry, [sc_scr[slot, (2 * h + u) * QB:(2 * h + u + 1) * QB, :] for u in range(2)],
                           lambda prs, v_pair=v_pair: dot(v_pair, jnp.concatenate(prs, axis=0)))
        stage_scores(it + 1, 1 - slot)
        return carry

    state = (jnp.full((1, JQ), NEG, F32), jnp.zeros((1, JQ), F32), jnp.zeros((HD, JQ), F32))
    stage_scores(0, 0)
    state = lax.fori_loop(0, (n_far + S - 1) // S, far_step, state)
    kt_prev = jnp.maximum(qb - 1, 0)
    s_prev = dot(key_tile(ks_scr, kt_prev), q_near) + bw_ref[0, jnp.where(qb >= 1, 1, N_WIN_TILES)]
    s_diag = dot(key_tile(ks_scr, qb), q_near) + bw_ref[0, 0]
    _, l_s, acc_s = online(state, [s_prev, s_diag],
                           lambda prs: dot(vs_scr[kt_prev], prs[0]) + dot(vs_scr[qb], prs[1]))
    o_s = acc_s * finish(l_s)

    scores, values = [], []
    for d in range(N_WIN_TILES):
        kt = jnp.maximum(qb - d, 0)
        start = pl.multiple_of(kt * QB, QB)
        scores.append(dot(kw_scr[pl.ds(start, QB), :], q_t) + bw_ref[0, jnp.where(qb >= d, d, N_WIN_TILES)])
        values.append(vw_scr[kt])
    es, inv_w = softmax_parts(scores)
    acc_w = functools.reduce(jnp.add, [dot(v_t, e.astype(BF16)) for v_t, e in zip(values, es)])
    o_w = acc_w * inv_w

    gate_scr[...] = _sigmoid(gate_ref[0].astype(F32)).T
    g_rows = gate_scr[pl.ds(pl.multiple_of(grp * NSA_GATE_STRIDE, NSA_GATE_STRIDE), NSA_GATE_STRIDE), :]
    g_c, g_s, g_w = (jnp.concatenate([g_rows[r * J + j:r * J + j + 1, :] for j in range(J)], axis=1)
                     for r in range(3))
    o_t = g_c * o_c + g_s * o_s + g_w * o_w
    o_ref[0] = jnp.concatenate([o_t[:, j * QB:(j + 1) * QB] for j in range(J)], axis=0).T


def _nsa(p, kvc, bias_c, bias_w, cover_t):
    Bsz, T, _ = p.shape
    G, J, HD, QB = B_GROUPS, B_J, B_HD, Q_BLOCK
    n_q = T // QB
    n_pad = kvc.shape[2]
    JQ = J * QB
    q_blk = COL_B * LANES // (J * HD)

    def kv_rows(first):
        return pl.BlockSpec((1, T, LANES), lambda b, g, i: (b, 0, COL_B + first + g))

    return pl.pallas_call(
        _nsa_kernel,
        grid=(Bsz, G, n_q),
        in_specs=[pl.BlockSpec((1, QB, J * HD), lambda b, g, i: (b, i, q_blk + g)),
                  pl.BlockSpec((1, QB, LANES), lambda b, g, i: (b, i, COL_B + NSA_GATE)),
                  pl.BlockSpec((1, 1, n_pad, LANES), lambda b, g, i: (b, g, 0, 0)),
                  kv_rows(NSA_SEL), kv_rows(NSA_WIN),
                  pl.BlockSpec((1, n_pad, JQ), lambda b, g, i: (g, 0, i)),
                  pl.BlockSpec((1, N_WIN_TILES + 1, QB, JQ), lambda b, g, i: (g, 0, 0, 0)),
                  pl.BlockSpec(cover_t.shape, lambda b, g, i: (0, 0))],
        out_specs=pl.BlockSpec((1, QB, J * HD), lambda b, g, i: (b, i, g)),
        out_shape=jax.ShapeDtypeStruct((Bsz, T, G * J * HD), F32),
        scratch_shapes=[pltpu.VMEM((T, 2 * LANES), BF16), pltpu.VMEM((T, LANES), BF16),
                        pltpu.VMEM((n_q, HD, QB), BF16), pltpu.VMEM((n_q, HD, QB), BF16),
                        pltpu.VMEM((LANES, QB), F32),
                        pltpu.VMEM((2, SEL_TILES_PER_STEP * QB, JQ), F32),
                        pltpu.VMEM((n_q // 2, HD, 2 * QB), BF16),
                        pltpu.VMEM((2, 2 * HD + LANES, JQ), BF16)],
        compiler_params=_cparams(("arbitrary", "arbitrary", "arbitrary")),
        name="nsa_attention",
    )(p, p, kvc, p, p, bias_c, bias_w, cover_t)


def _merge_kernel(ya_ref, yb_ref, yc_ref, ga_ref, gb_ref, gc_ref, x_ref, gt_ref, wb_ref, wo_ref, o_ref):
    merged = (_sigmoid(ga_ref[0].astype(F32)) * _mm(ya_ref[0], wb_ref[0])
              + _sigmoid(gb_ref[0].astype(F32)) * _mm(yb_ref[0], wb_ref[1])
              + _sigmoid(gc_ref[0].astype(F32)) * _mm(yc_ref[0], wb_ref[2]))
    o_ref[0] = x_ref[0] + gt_ref[0] * _mm(merged, wo_ref[...])


def _merge(ya, yb, yc, p, x, gt, wb, wo):
    Bsz, T, D = x.shape
    tm = 512
    bw = ya.shape[2]
    gate_blk = COL_GATE * LANES // D
    yspec = pl.BlockSpec((1, tm, bw), lambda b, i: (b, i, 0))

    def gspec(k):
        return pl.BlockSpec((1, tm, D), lambda b, i: (b, i, gate_blk + k))

    return pl.pallas_call(
        _merge_kernel,
        grid=(Bsz, T // tm),
        in_specs=[yspec, yspec, yspec, gspec(0), gspec(1), gspec(2),
                  pl.BlockSpec((1, tm, D), lambda b, i: (b, i, 0)),
                  pl.BlockSpec((1, 1, D), lambda b, i: (b, 0, 0)),
                  pl.BlockSpec(wb.shape, lambda b, i: (0, 0, 0)),
                  pl.BlockSpec(wo.shape, lambda b, i: (0, 0))],
        out_specs=pl.BlockSpec((1, tm, D), lambda b, i: (b, i, 0)),
        out_shape=jax.ShapeDtypeStruct(x.shape, F32),
        compiler_params=_cparams(("arbitrary", "arbitrary")),
        name="merge_out_proj",
    )(ya, yb, yc, p, p, p, x, gt, wb, wo)


def _ffn_kernel(final, x_ref, nw_ref, sh_ref, sc_ref, gt_ref, w1_ref, w3_ref, w2_ref, fw_ref, o_ref,
                h_scr, acc_scr):
    f = pl.program_id(2)

    @pl.when(f == 0)
    def _():
        h_scr[...] = _norm_mod(x_ref[0], nw_ref[...], sh_ref[0], sc_ref[0]).astype(BF16)
        acc_scr[...] = jnp.zeros_like(acc_scr)

    h = h_scr[...]
    u = _silu(jnp.dot(h, w1_ref[...], preferred_element_type=F32)) * jnp.dot(h, w3_ref[...],
                                                                           preferred_element_type=F32)
    acc_scr[...] += _mm(u, w2_ref[...])

    @pl.when(f == pl.num_programs(2) - 1)
    def _():
        x = x_ref[0] + gt_ref[0] * acc_scr[...]
        if final:
            x = x * lax.rsqrt(jnp.mean(x * x, axis=-1, keepdims=True) + 1e-6) * fw_ref[...]
        o_ref[0] = x


def _ffn(x, nw, sh, sc, gt, w1, w3, w2, fw, final):
    Bsz, T, D = x.shape
    tm, tf = min(T, 1024), 256
    n_ff = w1.shape[1]
    vec = pl.BlockSpec((1, D), lambda b, i, f: (0, 0))
    bvec = pl.BlockSpec((1, 1, D), lambda b, i, f: (b, 0, 0))
    xspec = pl.BlockSpec((1, tm, D), lambda b, i, f: (b, i, 0))
    return pl.pallas_call(
        functools.partial(_ffn_kernel, final),
        grid=(Bsz, T // tm, n_ff // tf),
        in_specs=[xspec, vec, bvec, bvec, bvec,
                  pl.BlockSpec((D, tf), lambda b, i, f: (0, f)),
                  pl.BlockSpec((D, tf), lambda b, i, f: (0, f)),
                  pl.BlockSpec((tf, D), lambda b, i, f: (f, 0)),
                  vec],
        out_specs=xspec,
        out_shape=jax.ShapeDtypeStruct(x.shape, F32),
        scratch_shapes=[pltpu.VMEM((tm, D), BF16), pltpu.VMEM((tm, D), F32)],
        compiler_params=_cparams(("arbitrary", "arbitrary", "arbitrary")),
        name="ffn_swiglu",
    )(x, nw, sh, sc, gt, w1, w3, w2, fw)


def _pack_w_in(w):
    a_in, b_in, c_in = 2048, 1304, 1792
    wa = w[:, :a_in]
    wb = w[:, a_in:a_in + b_in]
    wc = w[:, a_in + b_in:a_in + b_in + c_in]
    wg = w[:, a_in + b_in + c_in:]
    src, used = _nsa_columns()
    wb = jnp.where(used[None, :], jnp.take(wb, src, axis=1), 0.0)
    return jnp.concatenate([wa, wg, wc, wb], axis=1).astype(BF16)


def _nsa_columns():
    G, J, HD = B_GROUPS, B_J, B_HD
    width = NSA_BLOCKS * LANES
    src = np.zeros((width,), np.int32)
    used = np.zeros((width,), bool)
    n_q = G * J * HD
    src[:n_q] = np.arange(n_q)
    used[:n_q] = True
    for pair, blk in enumerate((NSA_CMP, NSA_SEL, NSA_WIN)):
        for g in range(G):
            for kv in range(2):
                dst = (blk + g) * LANES + kv * HD
                src[dst:dst + HD] = n_q + (2 * pair + kv) * G * HD + g * HD + np.arange(HD)
                used[dst:dst + HD] = True
    gate0 = n_q + 6 * G * HD
    for g in range(G):
        for j in range(J):
            for r in range(3):
                dst = NSA_GATE * LANES + g * NSA_GATE_STRIDE + r * J + j
                src[dst] = gate0 + (g * J + j) * 3 + r
                used[dst] = True
    return src, used


def _nsa_layer(p, pe_k, w1_k, w2_k, pe_v, w1_v, w2_v, bias_c, bias_w, cover):
    HD = B_HD
    w1k = w1_k.reshape(CMP_LEN, HD, CMP_HIDDEN)
    w1v = w1_v.reshape(CMP_LEN, HD, CMP_HIDDEN)
    zero1 = jnp.zeros_like(w1k)
    w1 = jnp.concatenate([jnp.concatenate([w1k, zero1], axis=2),
                          jnp.concatenate([zero1, w1v], axis=2)], axis=1).astype(BF16)
    zero2 = jnp.zeros_like(w2_k)
    w2 = jnp.concatenate([jnp.concatenate([w2_k, zero2], axis=1),
                          jnp.concatenate([zero2, w2_v], axis=1)], axis=0).astype(BF16)
    pe = jnp.concatenate([pe_k, pe_v], axis=1)
    kvc = _compress(p, pe, w1, w2)
    return _nsa(p, kvc, bias_c, bias_w, cover)


def _cover_matrix(T, n_pad):
    n = jnp.arange(n_pad)[None, :] * CMP_STRIDE
    m = jnp.arange(LANES)[:, None] * SEL_LEN
    real = (jnp.arange(n_pad)[None, :] < (T - CMP_LEN) // CMP_STRIDE + 1) & (jnp.arange(LANES)[:, None] < T // SEL_LEN)
    return ((n < m + SEL_LEN) & (n + CMP_LEN > m) & real).astype(BF16)


def kernel(x, c, ada_w, ada_b, norm1_w, norm2_w, w_in, hgrn_lb_logits, hgrn_norm_w, nsa_pe_k, nsa_cmp_w1_k, nsa_cmp_w2_k, nsa_pe_v, nsa_cmp_w1_v, nsa_cmp_w2_v, rel_bias, rw_mu, rw_w0, rw_w2, rw_a0, rw_a2, rw_g2, rw_k_k, rw_k_a, rw_r_k, rw_lnx_w, rw_lnx_b, w_branch, w_out, ffn_w1, ffn_w3, ffn_w2, final_norm_w):
    Bsz, T, D = x.shape
    L = ada_w.shape[0]
    n_pad = T // CMP_STRIDE
    c_pad = jnp.pad(c, ((0, 8 - Bsz), (0, 0)))
    ada = _ada(c_pad, ada_w, ada_b)[:, :Bsz]
    bias_c, bias_w = _bias_tables(rel_bias, T, n_pad)
    cover = _cover_matrix(T, n_pad)
    zeros64 = jnp.zeros((C_DECAY_LORA, C_HEADS * C_HD), F32)
    for l in range(L):
        sh1, sc1, gt1, sh2, sc2, gt2 = (ada[l, :, None, i * D:(i + 1) * D] for i in range(6))
        p = _inproj(x, norm1_w[l].reshape(1, D), sh1, sc1, _pack_w_in(w_in[l]))
        y_a = _hgrn(p, hgrn_lb_logits, hgrn_norm_w[l], l)
        y_b = _nsa_layer(p, nsa_pe_k[l], nsa_cmp_w1_k[l], nsa_cmp_w2_k[l],
                         nsa_pe_v[l], nsa_cmp_w1_v[l], nsa_cmp_w2_v[l], bias_c, bias_w, cover)
        y_c = _rwkv(p, rw_mu[l],
                    jnp.concatenate([rw_w2[l], zeros64], axis=0),
                    jnp.concatenate([zeros64, rw_a2[l]], axis=0),
                    rw_g2[l], rw_w0[l], rw_a0[l], rw_k_k[l], rw_k_a[l], rw_r_k[l], rw_lnx_w[l], rw_lnx_b[l])
        x = _merge(y_a, y_b, y_c, p, x, gt1, w_branch[l].astype(BF16), w_out[l].astype(BF16))
        x = _ffn(x, norm2_w[l].reshape(1, D), sh2, sc2, gt2,
                 ffn_w1[l].astype(BF16), ffn_w3[l].astype(BF16), ffn_w2[l].astype(BF16),
                 final_norm_w.reshape(1, D), l == L - 1)
    return x
```

```python
import functools
import math

import jax
import jax.numpy as jnp
import numpy as np
from jax import lax
from jax.experimental import pallas as pl
from jax.experimental.pallas import tpu as pltpu

F32 = jnp.float32
BF16 = jnp.bfloat16
I32 = jnp.int32

LANES = 128
VMEM_LIMIT = 56 * 1024 * 1024

D_MODEL = 1024
DEPTH = 4
A_HEADS, A_DK, A_DV = 4, 128, 128
A_CHUNK, A_SUB = 64, 16
B_HEADS, B_GROUPS, B_HD = 8, 2, 64
B_J = B_HEADS // B_GROUPS
CMP_LEN, CMP_STRIDE, CMP_HIDDEN = 32, 16, 256
SEL_LEN, SEL_TOP, N_LOCAL = 64, 16, 2
WINDOW, Q_BLOCK = 512, 128
N_WIN_TILES = WINDOW // Q_BLOCK + 1
SEL_TILES_PER_STEP = 2
REL_BUCKETS, REL_MAX_DIST = 32, 128
LOG2E = math.log2(math.e)
NSA_MASKED = -1e30
NSA_EMPTY = -1e29
C_HEADS, C_HD = 8, 64
C_CHUNK = 64
C_SUBS = 8
C_PACK = 4
C_DECAY_LORA, C_AAA_LORA, C_GATE_LORA = 64, 64, 128
LNX_EPS = 64e-5
D_FF = 2816

COL_A = 0
COL_GATE = 2048 // LANES
COL_C = 5120 // LANES
COL_B = 6912 // LANES
P_COLS = 8448
P_DTYPE = BF16
NSA_Q, NSA_CMP, NSA_SEL, NSA_WIN, NSA_GATE, NSA_BLOCKS = 0, 4, 6, 8, 10, 12
NSA_GATE_STRIDE = 16


def _cparams(sem):
    return pltpu.CompilerParams(dimension_semantics=sem, vmem_limit_bytes=VMEM_LIMIT)


def _split3(x):
    hi = x.astype(BF16)
    r1 = x - hi.astype(F32)
    mid = r1.astype(BF16)
    lo = (r1 - mid.astype(F32)).astype(BF16)
    return hi, mid, lo


def _mm(a, b):
    return jnp.dot(a.astype(BF16), b.astype(BF16), preferred_element_type=F32)


def _mm_nt(a, b):
    return lax.dot_general(a.astype(BF16), b.astype(BF16), (((1,), (1,)), ((), ())),
                           preferred_element_type=F32)


def _mm_tn(a, b):
    return lax.dot_general(a.astype(BF16), b.astype(BF16), (((0,), (0,)), ((), ())),
                           preferred_element_type=F32)


def _mm_exact_lhs(a_exact, b):
    a = a_exact.astype(BF16)
    hi, mid, lo = _split3(b)
    d = functools.partial(jnp.dot, preferred_element_type=F32)
    return d(a, hi) + d(a, mid) + d(a, lo)


def _mm_exact_rhs(a, b_exact):
    b = b_exact.astype(BF16)
    hi, mid, lo = _split3(a)
    d = functools.partial(jnp.dot, preferred_element_type=F32)
    return d(hi, b) + d(mid, b) + d(lo, b)


def _mm_x3(a, b):
    ah = a.astype(BF16)
    al = (a - ah.astype(F32)).astype(BF16)
    bh = b.astype(BF16)
    bl = (b - bh.astype(F32)).astype(BF16)
    d = functools.partial(jnp.dot, preferred_element_type=F32)
    return d(ah, bh) + d(ah, bl) + d(al, bh)


def _sigmoid(x):
    return 1.0 / (1.0 + jnp.exp(-x))


def _silu(x):
    return x * _sigmoid(x)


def _log_sigmoid(x):
    return jnp.minimum(x, 0.0) - jnp.log(1.0 + jnp.exp(-jnp.abs(x)))


def _iota(shape, dim):
    return lax.broadcasted_iota(I32, shape, dim)


def _ada_kernel(c_ref, w_ref, b_ref, o_ref):
    c = c_ref[...]
    o_ref[0] = _mm_x3(_silu(c), w_ref[0]) + b_ref[0]


def _ada(c_pad, ada_w, ada_b):
    L = ada_w.shape[0]
    n_out = ada_w.shape[2]
    tn = D_MODEL
    return pl.pallas_call(
        _ada_kernel,
        grid=(L, n_out // tn),
        in_specs=[pl.BlockSpec((8, D_MODEL), lambda l, j: (0, 0)),
                  pl.BlockSpec((1, D_MODEL, tn), lambda l, j: (l, 0, j)),
                  pl.BlockSpec((1, 1, tn), lambda l, j: (l, 0, j))],
        out_specs=pl.BlockSpec((1, 8, tn), lambda l, j: (l, 0, j)),
        out_shape=jax.ShapeDtypeStruct((L, 8, n_out), F32),
        compiler_params=_cparams(("arbitrary", "arbitrary")),
        name="ada_ln",
    )(c_pad, ada_w, ada_b.reshape(L, 1, n_out))


def _norm_mod(x, nw, sh, sc):
    y = x * lax.rsqrt(jnp.mean(x * x, axis=-1, keepdims=True) + 1e-6) * nw
    return y * (1.0 + sc) + sh


def _inproj_kernel(x_ref, nw_ref, sh_ref, sc_ref, w_ref, o_ref, h_scr):
    @pl.when(pl.program_id(2) == 0)
    def _():
        h_scr[...] = _norm_mod(x_ref[0], nw_ref[...], sh_ref[0], sc_ref[0]).astype(BF16)

    o_ref[0] = jnp.dot(h_scr[...], w_ref[...], preferred_element_type=F32).astype(o_ref.dtype)


def _inproj(x, nw, sh, sc, w):
    Bsz, T, D = x.shape
    n_out = w.shape[1]
    tm, tn = min(T, 1024), 768
    return pl.pallas_call(
        _inproj_kernel,
        grid=(Bsz, T // tm, n_out // tn),
        in_specs=[pl.BlockSpec((1, tm, D), lambda b, i, n: (b, i, 0)),
                  pl.BlockSpec((1, D), lambda b, i, n: (0, 0)),
                  pl.BlockSpec((1, 1, D), lambda b, i, n: (b, 0, 0)),
                  pl.BlockSpec((1, 1, D), lambda b, i, n: (b, 0, 0)),
                  pl.BlockSpec((D, tn), lambda b, i, n: (0, n))],
        out_specs=pl.BlockSpec((1, tm, tn), lambda b, i, n: (b, i, n)),
        out_shape=jax.ShapeDtypeStruct((Bsz, T, n_out), P_DTYPE),
        scratch_shapes=[pltpu.VMEM((tm, D), BF16)],
        compiler_params=_cparams(("arbitrary", "arbitrary", "arbitrary")),
        name="in_proj",
    )(x, nw, sh, sc, w)


def _hgrn_kernel(layer, lbl_ref, nw_ref, q_ref, f_ref, i_ref, g_ref, o_ref, s_scr):
    C, c = A_CHUNK, A_SUB
    slabs = [slice(h * A_DK, (h + 1) * A_DK) for h in range(A_HEADS)]

    @pl.when(pl.program_id(1) == 0)
    def _():
        s_scr[...] = jnp.zeros_like(s_scr)

    logits = lbl_ref[...]
    ex = jnp.exp(logits - jnp.max(logits, axis=0, keepdims=True))
    sm = ex / jnp.sum(ex, axis=0, keepdims=True)
    lb = jnp.zeros((1, logits.shape[1]), F32)
    for i in range(1, layer + 1):
        lb = lb + sm[i:i + 1, :]

    q = _silu(q_ref[0].astype(F32))
    fr = f_ref[0].astype(F32)
    v = i_ref[0].astype(F32)
    la = jnp.log(lb)
    lc = jnp.log(1.0 - lb) + _log_sigmoid(fr)
    log_f = jnp.maximum(la, lc) + jnp.log(1.0 + jnp.exp(-jnp.abs(la - lc)))
    k = (1.0 - lb) * _sigmoid(-fr)

    tri = (_iota((C, C), 0) >= _iota((C, C), 1)).astype(BF16)
    b = _mm_exact_lhs(tri, log_f) * LOG2E
    b_last = b[C - 1:C, :]

    qe = q * jnp.exp2(b)
    o = [_mm_nt(qe[:, sl], s_scr[h]) for h, sl in enumerate(slabs)]
    off = [[jnp.zeros((c, A_DV), F32)] for _ in slabs]
    for blk in range(1, C // c):
        lo = blk * c
        ref = b[lo - 1:lo, :]
        qi = q[lo:lo + c] * jnp.exp2(b[lo:lo + c] - ref)
        ki = k[:lo] * jnp.exp2(ref - b[:lo])
        att = [_mm_nt(qi[:, sl], ki[:, sl]) for sl in slabs]
        for h, sl in enumerate(slabs):
            off[h].append(_mm(att[h], v[:lo, sl]))
    o = [o[h] + jnp.concatenate(off[h], axis=0) for h in range(len(slabs))]
    row = _iota((C, 1), 0) % c
    for j in range(c):
        if j == 0:
            kj, bj, vj = k, b, v
        else:
            kj, bj, vj = (pltpu.roll(t, j, 0) for t in (k, b, v))
        ok = row >= j
        term = jnp.where(ok, q * kj * jnp.exp2(jnp.minimum(b - bj, 0.0)), 0.0)
        for h, sl in enumerate(slabs):
            o[h] = o[h] + jnp.sum(term[:, sl], axis=1, keepdims=True) * vj[:, sl]

    kd = k * jnp.exp2(b_last - b)
    p_end = jnp.exp2(b_last)
    for h, sl in enumerate(slabs):
        s_scr[h] = s_scr[h] * p_end[:, sl] + _mm_tn(v[:, sl], kd[:, sl])

    gate = _sigmoid(g_ref[0].astype(F32))
    for h, sl in enumerate(slabs):
        oh = o[h]
        oh = oh * lax.rsqrt(jnp.mean(oh * oh, axis=-1, keepdims=True) + 1e-5) * nw_ref[:, sl]
        o_ref[0, :, sl] = oh * gate[:, sl]


def _hgrn(p, lb_logits, norm_w, layer):
    Bsz, T, _ = p.shape
    C = A_CHUNK
    L = lb_logits.shape[0]
    W = A_HEADS * A_DK
    cw = COL_A * LANES // W

    def col(i):
        return pl.BlockSpec((1, C, W), lambda b, c: (b, c, cw + i))

    return pl.pallas_call(
        functools.partial(_hgrn_kernel, layer),
        grid=(Bsz, T // C),
        in_specs=[pl.BlockSpec((L, W), lambda b, c: (0, 0)),
                  pl.BlockSpec((1, A_HEADS * A_DV), lambda b, c: (0, 0)),
                  col(0), col(1), col(2), col(3)],
        out_specs=pl.BlockSpec((1, C, A_HEADS * A_DV), lambda b, c: (b, c, 0)),
        out_shape=jax.ShapeDtypeStruct((Bsz, T, A_HEADS * A_DV), F32),
        scratch_shapes=[pltpu.VMEM((A_HEADS, A_DV, A_DK), F32)],
        compiler_params=_cparams(("arbitrary", "arbitrary")),
        name="hgrn2",
    )(lb_logits, norm_w.reshape(1, -1), p, p, p, p)


def _rwkv_kernel(r_ref, k_ref, v_ref, wa_ref, xg_ref, mu_ref,
                 w2_ref, a2_ref, g2_ref, w0_ref, a0_ref, kk_ref, ka_ref, rk_ref, lnw_ref, lnb_ref,
                 o_ref, s_scr, pr_scr, pk_scr, pv_scr, pwa_scr, pg_scr):
    C = C_CHUNK
    W = C_HEADS * C_HD
    n_hp = W // LANES

    @pl.when(pl.program_id(1) == 0)
    def _():
        s_scr[...] = jnp.zeros_like(s_scr)
        for scr in (pr_scr, pk_scr, pv_scr, pwa_scr, pg_scr):
            scr[...] = jnp.zeros_like(scr)

    R = C_SUBS * C
    subs = [slice(s * C, (s + 1) * C) for s in range(C_SUBS)]
    row0 = _iota((R, 1), 0) == 0

    def shifted(ref, prev_scr, mu):
        cur = ref[0].astype(F32)
        prev = jnp.where(row0, prev_scr[...], pltpu.roll(cur, 1, 0))
        prev_scr[...] = cur[R - 1:R, :]
        return cur + (prev - cur) * mu

    r = shifted(r_ref, pr_scr, mu_ref[:, 0:W])
    k = shifted(k_ref, pk_scr, mu_ref[:, W:2 * W])
    v = shifted(v_ref, pv_scr, mu_ref[:, 2 * W:3 * W])
    xwa = shifted(wa_ref, pwa_scr, mu_ref[:, 3 * W:3 * W + LANES])
    xg = shifted(xg_ref, pg_scr, mu_ref[:, 3 * W + LANES:])

    w = w0_ref[...] + _mm_x3(jnp.tanh(xwa), w2_ref[...])
    sp = jnp.maximum(-w, 0.0) + jnp.log(1.0 + jnp.exp(-jnp.abs(w)))
    lw = -jnp.exp(-sp - 0.5)
    a = _sigmoid(a0_ref[...] + _mm_x3(xwa, a2_ref[...]))
    g = _mm_x3(_sigmoid(xg), g2_ref[...])

    ri, rj = _iota((R, R), 0), _iota((R, R), 1)
    tri_incl = ((ri >= rj) & (ri // C == rj // C)).astype(BF16)
    b = _mm_exact_lhs(tri_incl, lw)
    b_last = [b[rs.stop - 1:rs.stop, :] for rs in subs]
    p_end = [jnp.exp(bl) for bl in b_last]
    inv_p = jnp.exp(-b)
    e_end = jnp.concatenate([jnp.exp(bl - b[rs]) for bl, rs in zip(b_last, subs)], axis=0)
    kk_raw = k * kk_ref[...]
    k = k * (1.0 + (a - 1.0) * ka_ref[...])
    bonus_in = r * k * rk_ref[...]
    r_t = r * jnp.exp(b)
    p_prev = jnp.exp(b - lw)

    bd_f = ((_iota((LANES, LANES), 0) < C_HD) == (_iota((LANES, LANES), 1) < C_HD)).astype(F32)
    slabs = [slice(hp * LANES, (hp + 1) * LANES) for hp in range(n_hp)]
    dot = functools.partial(jnp.dot, preferred_element_type=F32)
    bd_b = bd_f.astype(BF16)

    PH = C_PACK
    PW = PH * C_HD
    n_packs = W // PW
    packs = [slice(i * PW, (i + 1) * PW) for i in range(n_packs)]
    s_lane = _iota((C, PH * C), 1) % C
    t_row = _iota((C, PH * C), 0)
    lower_strict = t_row > s_lane
    lower_incl = t_row >= s_lane
    same_head = (_iota((PH * C, PW), 0) // C) == (_iota((PH * C, PW), 1) // C_HD)
    state_mask = (_iota((PW, PW), 0) // C_HD) == (_iota((PW, PW), 1) // C_HD)

    def spread(x):
        xb = x.astype(BF16)
        return jnp.where(same_head, jnp.concatenate([xb] * PH, axis=0), jnp.zeros((), BF16))

    def group_sums(x):
        st = jnp.concatenate([x[:, sl] for sl in slabs], axis=0)
        hi = st.astype(BF16)
        lo = (st - hi.astype(F32)).astype(BF16)
        out = dot(hi, bd_b) + dot(lo, bd_b)
        return jnp.concatenate([out[i * R:(i + 1) * R] for i in range(n_hp)], axis=1)

    kk = kk_raw * lax.rsqrt(jnp.maximum(group_sums(kk_raw * kk_raw), 1e-24))
    al = a * kk
    bonus = group_sums(bonus_in) * v
    kap = kk * p_prev
    al_h = al * inv_p
    k_h = k * inv_p
    al_e = al * e_end
    k_e = k * e_end

    PC = PH * C
    units = [(rs, pk) for rs in subs for pk in packs]
    lhs = [jnp.concatenate([kap[rs, pk], r_t[rs, pk]], axis=0).astype(BF16) for rs, pk in units]
    e_st = [jnp.concatenate([al_e[rs, pk], k_e[rs, pk]], axis=0).astype(BF16) for rs, pk in units]
    v_b = [v[rs, pk].astype(BF16) for rs, pk in units]
    v_sp = [spread(v[rs, pk]) for rs, pk in units]
    big = [_mm_nt(lhs[i], jnp.concatenate([spread(al_h[rs, pk]), spread(k_h[rs, pk])], axis=0))
           for i, (rs, pk) in enumerate(units)]
    a_ak = [jnp.where(lower_strict, m[:C, :PC], 0.0) for m in big]
    b_al = [jnp.where(lower_incl, m[C:, :PC], 0.0) for m in big]
    kv = [dot(jnp.concatenate([jnp.where(lower_strict, m[:C, PC:], 0.0),
                               jnp.where(lower_incl, m[C:, PC:], 0.0)], axis=0).astype(BF16), v_sp[i])
          for i, m in enumerate(big)]

    eye = (t_row == s_lane).astype(F32)
    n = [-m for m in a_ak]
    x = [eye + m for m in n]
    n = [dot(m.astype(BF16), spread(m)) for m in n]
    power = 2
    while 2 * power < C:
        for i in range(len(units)):
            res = dot(jnp.concatenate([x[i], n[i]], axis=0).astype(BF16), spread(n[i]))
            x[i] = x[i] + res[:C]
            n[i] = res[C:]
        power *= 2
    x = [x[i] + dot(x[i].astype(BF16), spread(n[i])) for i in range(len(units))]

    state = [s_scr[i] for i in range(n_packs)]
    y_rows = []
    for si in range(C_SUBS):
        y_parts = []
        for pi, pk in enumerate(packs):
            i = si * n_packs + pi
            ks = _mm_nt(lhs[i], state[pi])
            u = -dot(x[i].astype(BF16), spread(ks[:C] + kv[i][:C]))
            y_parts.append(ks[C:] + dot(b_al[i].astype(BF16), spread(u)) + kv[i][C:])
            upd = _mm_tn(jnp.concatenate([u.astype(BF16), v_b[i]], axis=0), e_st[i])
            state[pi] = state[pi] * p_end[si][:, pk] + jnp.where(state_mask, upd, 0.0)
        y_rows.append(jnp.concatenate(y_parts, axis=1))
    for pi in range(n_packs):
        s_scr[pi] = state[pi]

    y = jnp.concatenate(y_rows, axis=0)
    inv_hd = 1.0 / C_HD
    yc = y - group_sums(y) * inv_hd
    var = group_sums(yc * yc) * inv_hd
    y = yc * lax.rsqrt(var + LNX_EPS) * lnw_ref[...] + lnb_ref[...]
    o_ref[0] = (y + bonus) * g


def _rwkv(p, mu, w2p, a2p, g2, w0, a0, k_k, k_a, r_k, lnx_w, lnx_b):
    Bsz, T, _ = p.shape
    C = C_SUBS * C_CHUNK
    W = C_HEADS * C_HD
    n_hp = W // LANES
    cw = COL_C * LANES // W
    pw = C_PACK * C_HD
    assert C_CHUNK == C_HD and W % pw == 0

    def wide(i):
        return pl.BlockSpec((1, C, W), lambda b, c: (b, c, cw + i))

    def narrow(i):
        return pl.BlockSpec((1, C, LANES), lambda b, c: (b, c, COL_C + 3 * n_hp + i))

    def const(shape):
        return pl.BlockSpec(shape, lambda b, c: (0,) * len(shape))

    row = lambda t: t.reshape(1, -1)
    vecs = [row(t) for t in (w0, a0, k_k, k_a, r_k, lnx_w, lnx_b)]
    return pl.pallas_call(
        _rwkv_kernel,
        grid=(Bsz, T // C),
        in_specs=[wide(0), wide(1), wide(2), narrow(0), narrow(1), const((1, mu.shape[0])),
                  const(w2p.shape), const(a2p.shape), const(g2.shape)] + [const((1, W))] * 7,
        out_specs=pl.BlockSpec((1, C, W), lambda b, c: (b, c, 0)),
        out_shape=jax.ShapeDtypeStruct((Bsz, T, W), F32),
        scratch_shapes=[pltpu.VMEM((W // pw, pw, pw), F32)] + [pltpu.VMEM((1, W), F32)] * 3
                       + [pltpu.VMEM((1, LANES), F32)] * 2,
        compiler_params=_cparams(("arbitrary", "arbitrary")),
        name="rwkv7",
    )(p, p, p, p, p, row(mu), w2p, a2p, g2, *vecs)


def _t5_thresholds():
    max_exact = REL_BUCKETS // 2
    out = []
    for kk in range(1, REL_BUCKETS - max_exact):
        n = max_exact
        while int(math.log(n / max_exact) / math.log(REL_MAX_DIST / max_exact) * (REL_BUCKETS - max_exact)) < kk:
            n += 1
        out.append(n)
    return out


def _bias_from_dist(dist, rb_ref, head, valid):
    n = jnp.maximum(dist, 0)
    max_exact = REL_BUCKETS // 2
    large = jnp.full(n.shape, max_exact, I32)
    for thr in _t5_thresholds():
        large = large + (n >= thr).astype(I32)
    bucket = jnp.where(n < max_exact, n, large)
    out = jnp.zeros(n.shape, F32)
    for j in range(REL_BUCKETS):
        out = jnp.where(bucket == j, rb_ref[j, head] * LOG2E, out)
    return jnp.where(valid, out, NSA_MASKED)


def _bias_cmp_kernel(rb_ref, o_ref):
    g, qb = pl.program_id(0), pl.program_id(1)
    n_pad = o_ref.shape[1]
    t = qb * Q_BLOCK + _iota((n_pad, Q_BLOCK), 1)
    n = _iota((n_pad, Q_BLOCK), 0)
    dist = t - (n * CMP_STRIDE + CMP_LEN - 1)
    valid = (dist >= 0) & (n < n_pad - 1)
    for j in range(B_J):
        o_ref[0, :, j * Q_BLOCK:(j + 1) * Q_BLOCK] = _bias_from_dist(dist, rb_ref, g * B_J + j, valid)


def _bias_win_kernel(rb_ref, o_ref):
    g, d = pl.program_id(0), pl.program_id(1)
    dist = d * Q_BLOCK + _iota((Q_BLOCK, Q_BLOCK), 1) - _iota((Q_BLOCK, Q_BLOCK), 0)
    valid = (dist >= 0) & (dist < WINDOW) & (d < N_WIN_TILES)
    for j in range(B_J):
        o_ref[0, 0, :, j * Q_BLOCK:(j + 1) * Q_BLOCK] = _bias_from_dist(dist, rb_ref, g * B_J + j, valid)


def _bias_tables(rel_bias, T, n_pad):
    smem = pl.BlockSpec(memory_space=pltpu.SMEM)
    jq = B_J * Q_BLOCK
    n_q = T // Q_BLOCK
    bias_c = pl.pallas_call(
        _bias_cmp_kernel,
        grid=(B_GROUPS, n_q),
        in_specs=[smem],
        out_specs=pl.BlockSpec((1, n_pad, jq), lambda g, i: (g, 0, i)),
        out_shape=jax.ShapeDtypeStruct((B_GROUPS, n_pad, n_q * jq), F32),
        compiler_params=_cparams(("arbitrary", "arbitrary")),
        name="nsa_bias_cmp",
    )(rel_bias)
    bias_w = pl.pallas_call(
        _bias_win_kernel,
        grid=(B_GROUPS, N_WIN_TILES + 1),
        in_specs=[smem],
        out_specs=pl.BlockSpec((1, 1, Q_BLOCK, jq), lambda g, d: (g, d, 0, 0)),
        out_shape=jax.ShapeDtypeStruct((B_GROUPS, N_WIN_TILES + 1, Q_BLOCK, jq), F32),
        compiler_params=_cparams(("arbitrary", "arbitrary")),
        name="nsa_bias_win",
    )(rel_bias)
    return bias_c, bias_w


def _compress_kernel(x_ref, pe_ref, w1_ref, w2_ref, o_ref, xs_scr):
    n_grp = o_ref.shape[2]
    hidden = w1_ref.shape[2]
    ha = jnp.zeros((n_grp, hidden), F32)
    hb = jnp.zeros((n_grp, hidden), F32)
    xs_scr[...] = x_ref[0].astype(F32)
    for tau in range(CMP_STRIDE):
        x_tau = xs_scr[pl.ds(tau, n_grp, stride=CMP_STRIDE), :]
        ha = ha + _mm(x_tau + pe_ref[tau:tau + 1, :], w1_ref[tau])
        hb = hb + _mm(x_tau + pe_ref[CMP_STRIDE + tau:CMP_STRIDE + tau + 1, :], w1_ref[CMP_STRIDE + tau])
    h = ha + pltpu.roll(hb, n_grp - 1, 0)
    out = _mm(_silu(h), w2_ref[...])
    o_ref[0, 0] = jnp.where(_iota(out.shape, 0) < n_grp - 1, out, 0.0)


def _compress(p, pe, w1, w2):
    Bsz, T, _ = p.shape
    n_grp = T // CMP_STRIDE
    return pl.pallas_call(
        _compress_kernel,
        grid=(Bsz, B_GROUPS),
        in_specs=[pl.BlockSpec((1, T, LANES), lambda b, g: (b, 0, COL_B + NSA_CMP + g)),
                  pl.BlockSpec(pe.shape, lambda b, g: (0, 0)),
                  pl.BlockSpec(w1.shape, lambda b, g: (0, 0, 0)),
                  pl.BlockSpec(w2.shape, lambda b, g: (0, 0))],
        out_specs=pl.BlockSpec((1, 1, n_grp, LANES), lambda b, g: (b, g, 0, 0)),
        out_shape=jax.ShapeDtypeStruct((Bsz, B_GROUPS, n_grp, LANES), F32),
        scratch_shapes=[pltpu.VMEM((T, LANES), F32)],
        compiler_params=_cparams(("arbitrary", "arbitrary")),
        name="nsa_compress",
    )(p, pe, w1, w2)


def _nsa_kernel(q_ref, gate_ref, kvc_ref, ks_ref, kw_ref, bc_ref, bw_ref, cov_ref, o_ref,
                ks_scr, kw_scr, vs_scr, vw_scr, gate_scr, sc0_scr, sc1_scr, vs2_scr, qf_scr):
    QB, J, HD = Q_BLOCK, B_J, B_HD
    JQ = J * QB
    grp = pl.program_id(1)
    qb = pl.program_id(2)
    t0 = qb * QB
    n_pad = kvc_ref.shape[2]
    n_tiles = ks_ref.shape[1] // QB
    NEG = -jnp.inf
    MASKED, EMPTY = NSA_MASKED, NSA_EMPTY
    dot = functools.partial(jnp.dot, preferred_element_type=F32)

    @pl.when(qb == 0)
    def _():
        lane = _iota((1, LANES), 1)
        key_blk = _iota((QB, 1), 0) // SEL_LEN

        def fill(pair, carry):
            for half in range(2):
                kt = 2 * pair + half
                rows = pl.ds(pl.multiple_of(kt * QB, QB), QB)
                tile = ks_ref[0, rows, :].astype(F32)
                k_ones = jnp.where(lane < HD, tile, jnp.where(lane < HD + 2, 1.0, 0.0))
                one_hot = (lane == key_blk + kt * (QB // SEL_LEN)).astype(F32)
                ks_scr[rows, :] = jnp.concatenate([k_ones, one_hot], axis=1).astype(BF16)
                v_t = tile.T[HD:, :].astype(BF16)
                vs_scr[kt] = v_t
                vs2_scr[pair, :, half * QB:(half + 1) * QB] = v_t
                tile = kw_ref[0, rows, :].astype(F32)
                kw_scr[rows, :] = tile.astype(BF16)
                vw_scr[kt] = tile.T[HD:, :].astype(BF16)
            return carry
        lax.fori_loop(0, n_tiles // 2, fill, 0)

    q_rows = (q_ref[0].astype(F32) * (HD ** -0.5 * LOG2E)).T
    q_t = jnp.concatenate([q_rows[j * HD:(j + 1) * HD] for j in range(J)], axis=1)
    q_t = jnp.concatenate([q_t, jnp.zeros_like(q_t)], axis=0).astype(BF16)
    i_lane = _iota((1, JQ), 1) % QB
    t_lane = t0 + i_lane
    t_q = t0 + _iota((1, QB), 1)

    def finish(e_sum):
        return 1.0 / jnp.maximum(e_sum, 1e-30)

    def softmax_parts(scores):
        m = functools.reduce(jnp.maximum, [jnp.max(sc, axis=0, keepdims=True) for sc in scores])
        m = jnp.where(m < EMPTY, 0.0, m)
        es = [jnp.exp2(sc - m) for sc in scores]
        tot = functools.reduce(jnp.add, [jnp.sum(e, axis=0, keepdims=True) for e in es])
        return es, finish(tot)

    kvc = kvc_ref[0, 0].astype(BF16)
    (e_c,), inv_c = softmax_parts([dot(kvc, q_t) + bc_ref[0]])
    p_c = e_c * inv_c
    o_c = _mm_tn(kvc, p_c)[HD:]

    p_sum = p_c[:, 0:QB]
    for j in range(1, J):
        p_sum = p_sum + p_c[:, j * QB:(j + 1) * QB]
    ph = p_sum.astype(BF16)
    pl_ = (p_sum - ph.astype(F32)).astype(BF16)
    cov = cov_ref[...]
    imp = dot(cov, ph) + dot(cov, pl_)
    n_blk = cov.shape[0]
    m_col = _iota((n_blk, 1), 0)
    cur = t_q // SEL_LEN
    causal = m_col <= cur
    forced = (m_col == 0) | ((m_col > cur - N_LOCAL) & causal)
    score = jnp.where(causal, jnp.where(forced, jnp.inf, imp), NEG)
    n_real = ks_ref.shape[1] // SEL_LEN
    st = score[:n_real, :]
    SUB = 8
    row_in_group = _iota((SUB, 1), 0)
    groups = [st[lo:lo + SUB, :] for lo in range(0, n_real, SUB)]
    cnts = [jnp.zeros((SUB, QB), F32) for _ in groups]
    for mp in range(n_real):
        other = st[mp:mp + 1, :]
        for gi, rows in enumerate(groups):
            lo = gi * SUB
            if lo > mp:
                beats = other >= rows
            elif lo + SUB - 1 < mp:
                beats = other > rows
            else:
                beats = (other > rows) | ((other == rows) & (row_in_group > mp - lo))
            cnts[gi] = cnts[gi] + beats.astype(F32)
    selected = jnp.concatenate(cnts, axis=0) < float(SEL_TOP)

    key_col = _iota((QB, 1), 0)
    q_rows = q_t[:HD].astype(F32)
    mask_rows = jnp.where(selected, 0.0, MASKED)
    if n_real < n_blk:
        mask_rows = jnp.concatenate([mask_rows, jnp.full((n_blk - n_real, QB), MASKED, F32)], axis=0)
    mask_rows = jnp.concatenate([mask_rows] * J, axis=1)
    far_bias = bw_ref[0, 2, 0:1, :]
    fb_hi = far_bias.astype(BF16).astype(F32)
    row = _iota((HD, 1), 0)
    bias_rows = jnp.where(row == 0, fb_hi, jnp.where(row == 1, far_bias - fb_hi, 0.0))
    qf_scr[0] = jnp.concatenate([q_rows, bias_rows, mask_rows], axis=0).astype(BF16)
    qf_scr[1] = jnp.concatenate([jnp.zeros((2 * HD, JQ), F32), jnp.full((n_blk, JQ), MASKED, F32)],
                                axis=0).astype(BF16)
    q_near = jnp.concatenate([q_rows, jnp.zeros_like(bias_rows), mask_rows], axis=0).astype(BF16)

    def key_tile(scr, kt):
        return scr[pl.ds(pl.multiple_of(kt * QB, QB), QB), :]

    def online(carry, scores, pv):
        m_old, l_old, acc = carry
        m_new = functools.reduce(jnp.maximum, [m_old] + [jnp.max(sc, axis=0, keepdims=True) for sc in scores])
        m_use = jnp.where(m_new < EMPTY, 0.0, m_new)
        alpha = jnp.exp2(m_old - m_use)
        prs = [jnp.exp2(sc - m_use) for sc in scores]
        l_new = alpha * l_old + functools.reduce(jnp.add, [jnp.sum(pr, axis=0, keepdims=True) for pr in prs])
        return m_new, l_new, alpha * acc + pv([pr.astype(BF16) for pr in prs])

    S = SEL_TILES_PER_STEP
    n_far = jnp.maximum(qb - 1, 0)

    def stage_scores(group, buf):
        for u in range(S):
            kt = group * S + u
            rhs = qf_scr[jnp.where(kt < n_far, 0, 1)]
            buf[u * QB:(u + 1) * QB, :] = dot(key_tile(ks_scr, jnp.minimum(kt, n_tiles - 1)), rhs)

    def process(group, buf, carry):
        for h in range(S // 2):
            v_pair = vs2_scr[jnp.minimum(group * (S // 2) + h, n_tiles // 2 - 1)]
            carry = online(carry, [buf[(2 * h + u) * QB:(2 * h + u + 1) * QB, :] for u in range(2)],
                           lambda prs, v_pair=v_pair: dot(v_pair, jnp.concatenate(prs, axis=0)))
        return carry

    def far_step(it, carry):
        group = 2 * it
        stage_scores(group + 1, sc1_scr)
        carry = process(group, sc0_scr, carry)
        stage_scores(group + 2, sc0_scr)
        return process(group + 1, sc1_scr, carry)

    state = (jnp.full((1, JQ), NEG, F32), jnp.zeros((1, JQ), F32), jnp.zeros((HD, JQ), F32))
    stage_scores(0, sc0_scr)
    n_groups = (n_far + S - 1) // S
    state = lax.fori_loop(0, (n_groups + 1) // 2, far_step, state)
    kt_prev = jnp.maximum(qb - 1, 0)
    s_prev = dot(key_tile(ks_scr, kt_prev), q_near) + bw_ref[0, jnp.where(qb >= 1, 1, N_WIN_TILES)]
    s_diag = dot(key_tile(ks_scr, qb), q_near) + bw_ref[0, 0]
    _, l_s, acc_s = online(state, [s_prev, s_diag],
                           lambda prs: dot(vs_scr[kt_prev], prs[0]) + dot(vs_scr[qb], prs[1]))
    o_s = acc_s * finish(l_s)

    scores, values = [], []
    for d in range(N_WIN_TILES):
        kt = jnp.maximum(qb - d, 0)
        start = pl.multiple_of(kt * QB, QB)
        scores.append(dot(kw_scr[pl.ds(start, QB), :], q_t) + bw_ref[0, jnp.where(qb >= d, d, N_WIN_TILES)])
        values.append(vw_scr[kt])
    es, inv_w = softmax_parts(scores)
    acc_w = functools.reduce(jnp.add, [dot(v_t, e.astype(BF16)) for v_t, e in zip(values, es)])
    o_w = acc_w * inv_w

    gate_scr[...] = _sigmoid(gate_ref[0].astype(F32)).T
    g_rows = gate_scr[pl.ds(pl.multiple_of(grp * NSA_GATE_STRIDE, NSA_GATE_STRIDE), NSA_GATE_STRIDE), :]
    g_c, g_s, g_w = (jnp.concatenate([g_rows[r * J + j:r * J + j + 1, :] for j in range(J)], axis=1)
                     for r in range(3))
    o_t = g_c * o_c + g_s * o_s + g_w * o_w
    o_ref[0] = jnp.concatenate([o_t[:, j * QB:(j + 1) * QB] for j in range(J)], axis=0).T


def _nsa(p, kvc, bias_c, bias_w, cover_t):
    Bsz, T, _ = p.shape
    G, J, HD, QB = B_GROUPS, B_J, B_HD, Q_BLOCK
    n_q = T // QB
    n_pad = kvc.shape[2]
    JQ = J * QB
    q_blk = COL_B * LANES // (J * HD)

    def kv_rows(first):
        return pl.BlockSpec((1, T, LANES), lambda b, g, i: (b, 0, COL_B + first + g))

    return pl.pallas_call(
        _nsa_kernel,
        grid=(Bsz, G, n_q),
        in_specs=[pl.BlockSpec((1, QB, J * HD), lambda b, g, i: (b, i, q_blk + g)),
                  pl.BlockSpec((1, QB, LANES), lambda b, g, i: (b, i, COL_B + NSA_GATE)),
                  pl.BlockSpec((1, 1, n_pad, LANES), lambda b, g, i: (b, g, 0, 0)),
                  kv_rows(NSA_SEL), kv_rows(NSA_WIN),
                  pl.BlockSpec((1, n_pad, JQ), lambda b, g, i: (g, 0, i)),
                  pl.BlockSpec((1, N_WIN_TILES + 1, QB, JQ), lambda b, g, i: (g, 0, 0, 0)),
                  pl.BlockSpec(cover_t.shape, lambda b, g, i: (0, 0))],
        out_specs=pl.BlockSpec((1, QB, J * HD), lambda b, g, i: (b, i, g)),
        out_shape=jax.ShapeDtypeStruct((Bsz, T, G * J * HD), F32),
        scratch_shapes=[pltpu.VMEM((T, 2 * LANES), BF16), pltpu.VMEM((T, LANES), BF16),
                        pltpu.VMEM((n_q, HD, QB), BF16), pltpu.VMEM((n_q, HD, QB), BF16),
                        pltpu.VMEM((LANES, QB), F32),
                        pltpu.VMEM((SEL_TILES_PER_STEP * QB, JQ), F32),
                        pltpu.VMEM((SEL_TILES_PER_STEP * QB, JQ), F32),
                        pltpu.VMEM((n_q // 2, HD, 2 * QB), BF16),
                        pltpu.VMEM((2, 2 * HD + LANES, JQ), BF16)],
        compiler_params=_cparams(("arbitrary", "arbitrary", "arbitrary")),
        name="nsa_attention",
    )(p, p, kvc, p, p, bias_c, bias_w, cover_t)


def _merge_kernel(ya_ref, yb_ref, yc_ref, ga_ref, gb_ref, gc_ref, x_ref, gt_ref, wb_ref, wo_ref, o_ref):
    merged = (_sigmoid(ga_ref[0].astype(F32)) * _mm(ya_ref[0], wb_ref[0])
              + _sigmoid(gb_ref[0].astype(F32)) * _mm(yb_ref[0], wb_ref[1])
              + _sigmoid(gc_ref[0].astype(F32)) * _mm(yc_ref[0], wb_ref[2]))
    o_ref[0] = x_ref[0] + gt_ref[0] * _mm(merged, wo_ref[...])


def _merge(ya, yb, yc, p, x, gt, wb, wo):
    Bsz, T, D = x.shape
    tm = 512
    bw = ya.shape[2]
    gate_blk = COL_GATE * LANES // D
    yspec = pl.BlockSpec((1, tm, bw), lambda b, i: (b, i, 0))

    def gspec(k):
        return pl.BlockSpec((1, tm, D), lambda b, i: (b, i, gate_blk + k))

    return pl.pallas_call(
        _merge_kernel,
        grid=(Bsz, T // tm),
        in_specs=[yspec, yspec, yspec, gspec(0), gspec(1), gspec(2),
                  pl.BlockSpec((1, tm, D), lambda b, i: (b, i, 0)),
                  pl.BlockSpec((1, 1, D), lambda b, i: (b, 0, 0)),
                  pl.BlockSpec(wb.shape, lambda b, i: (0, 0, 0)),
                  pl.BlockSpec(wo.shape, lambda b, i: (0, 0))],
        out_specs=pl.BlockSpec((1, tm, D), lambda b, i: (b, i, 0)),
        out_shape=jax.ShapeDtypeStruct(x.shape, F32),
        compiler_params=_cparams(("arbitrary", "arbitrary")),
        name="merge_out_proj",
    )(ya, yb, yc, p, p, p, x, gt, wb, wo)


def _ffn_kernel(final, x_ref, nw_ref, sh_ref, sc_ref, gt_ref, w1_ref, w3_ref, w2_ref, fw_ref, o_ref,
                h_scr, acc_scr):
    f = pl.program_id(2)

    @pl.when(f == 0)
    def _():
        h_scr[...] = _norm_mod(x_ref[0], nw_ref[...], sh_ref[0], sc_ref[0]).astype(BF16)
        acc_scr[...] = jnp.zeros_like(acc_scr)

    h = h_scr[...]
    u = _silu(jnp.dot(h, w1_ref[...], preferred_element_type=F32)) * jnp.dot(h, w3_ref[...],
                                                                           preferred_element_type=F32)
    acc_scr[...] += _mm(u, w2_ref[...])

    @pl.when(f == pl.num_programs(2) - 1)
    def _():
        x = x_ref[0] + gt_ref[0] * acc_scr[...]
        if final:
            x = x * lax.rsqrt(jnp.mean(x * x, axis=-1, keepdims=True) + 1e-6) * fw_ref[...]
        o_ref[0] = x


def _ffn(x, nw, sh, sc, gt, w1, w3, w2, fw, final):
    Bsz, T, D = x.shape
    tm, tf = min(T, 1024), 256
    n_ff = w1.shape[1]
    vec = pl.BlockSpec((1, D), lambda b, i, f: (0, 0))
    bvec = pl.BlockSpec((1, 1, D), lambda b, i, f: (b, 0, 0))
    xspec = pl.BlockSpec((1, tm, D), lambda b, i, f: (b, i, 0))
    return pl.pallas_call(
        functools.partial(_ffn_kernel, final),
        grid=(Bsz, T // tm, n_ff // tf),
        in_specs=[xspec, vec, bvec, bvec, bvec,
                  pl.BlockSpec((D, tf), lambda b, i, f: (0, f)),
                  pl.BlockSpec((D, tf), lambda b, i, f: (0, f)),
                  pl.BlockSpec((tf, D), lambda b, i, f: (f, 0)),
                  vec],
        out_specs=xspec,
        out_shape=jax.ShapeDtypeStruct(x.shape, F32),
        scratch_shapes=[pltpu.VMEM((tm, D), BF16), pltpu.VMEM((tm, D), F32)],
        compiler_params=_cparams(("arbitrary", "arbitrary", "arbitrary")),
        name="ffn_swiglu",
    )(x, nw, sh, sc, gt, w1, w3, w2, fw)


def _pack_w_in(w):
    a_in, b_in, c_in = 2048, 1304, 1792
    wa = w[:, :a_in]
    wb = w[:, a_in:a_in + b_in]
    wc = w[:, a_in + b_in:a_in + b_in + c_in]
    wg = w[:, a_in + b_in + c_in:]
    src, used = _nsa_columns()
    wb = jnp.where(used[None, :], jnp.take(wb, src, axis=1), 0.0)
    return jnp.concatenate([wa, wg, wc, wb], axis=1).astype(BF16)


def _nsa_columns():
    G, J, HD = B_GROUPS, B_J, B_HD
    width = NSA_BLOCKS * LANES
    src = np.zeros((width,), np.int32)
    used = np.zeros((width,), bool)
    n_q = G * J * HD
    src[:n_q] = np.arange(n_q)
    used[:n_q] = True
    for pair, blk in enumerate((NSA_CMP, NSA_SEL, NSA_WIN)):
        for g in range(G):
            for kv in range(2):
                dst = (blk + g) * LANES + kv * HD
                src[dst:dst + HD] = n_q + (2 * pair + kv) * G * HD + g * HD + np.arange(HD)
                used[dst:dst + HD] = True
    gate0 = n_q + 6 * G * HD
    for g in range(G):
        for j in range(J):
            for r in range(3):
                dst = NSA_GATE * LANES + g * NSA_GATE_STRIDE + r * J + j
                src[dst] = gate0 + (g * J + j) * 3 + r
                used[dst] = True
    return src, used


def _nsa_layer(p, pe_k, w1_k, w2_k, pe_v, w1_v, w2_v, bias_c, bias_w, cover):
    HD = B_HD
    w1k = w1_k.reshape(CMP_LEN, HD, CMP_HIDDEN)
    w1v = w1_v.reshape(CMP_LEN, HD, CMP_HIDDEN)
    zero1 = jnp.zeros_like(w1k)
    w1 = jnp.concatenate([jnp.concatenate([w1k, zero1], axis=2),
                          jnp.concatenate([zero1, w1v], axis=2)], axis=1).astype(BF16)
    zero2 = jnp.zeros_like(w2_k)
    w2 = jnp.concatenate([jnp.concatenate([w2_k, zero2], axis=1),
                          jnp.concatenate([zero2, w2_v], axis=1)], axis=0).astype(BF16)
    pe = jnp.concatenate([pe_k, pe_v], axis=1)
    kvc = _compress(p, pe, w1, w2)
    return _nsa(p, kvc, bias_c, bias_w, cover)


def _cover_matrix(T, n_pad):
    n = jnp.arange(n_pad)[None, :] * CMP_STRIDE
    m = jnp.arange(LANES)[:, None] * SEL_LEN
    real = (jnp.arange(n_pad)[None, :] < (T - CMP_LEN) // CMP_STRIDE + 1) & (jnp.arange(LANES)[:, None] < T // SEL_LEN)
    return ((n < m + SEL_LEN) & (n + CMP_LEN > m) & real).astype(BF16)


def kernel(x, c, ada_w, ada_b, norm1_w, norm2_w, w_in, hgrn_lb_logits, hgrn_norm_w, nsa_pe_k, nsa_cmp_w1_k, nsa_cmp_w2_k, nsa_pe_v, nsa_cmp_w1_v, nsa_cmp_w2_v, rel_bias, rw_mu, rw_w0, rw_w2, rw_a0, rw_a2, rw_g2, rw_k_k, rw_k_a, rw_r_k, rw_lnx_w, rw_lnx_b, w_branch, w_out, ffn_w1, ffn_w3, ffn_w2, final_norm_w):
    Bsz, T, D = x.shape
    L = ada_w.shape[0]
    n_pad = T // CMP_STRIDE
    c_pad = jnp.pad(c, ((0, 8 - Bsz), (0, 0)))
    ada = _ada(c_pad, ada_w, ada_b)[:, :Bsz]
    bias_c, bias_w = _bias_tables(rel_bias, T, n_pad)
    cover = _cover_matrix(T, n_pad)
    zeros64 = jnp.zeros((C_DECAY_LORA, C_HEADS * C_HD), F32)
    for l in range(L):
        sh1, sc1, gt1, sh2, sc2, gt2 = (ada[l, :, None, i * D:(i + 1) * D] for i in range(6))
        p = _inproj(x, norm1_w[l].reshape(1, D), sh1, sc1, _pack_w_in(w_in[l]))
        y_a = _hgrn(p, hgrn_lb_logits, hgrn_norm_w[l], l)
        y_b = _nsa_layer(p, nsa_pe_k[l], nsa_cmp_w1_k[l], nsa_cmp_w2_k[l],
                         nsa_pe_v[l], nsa_cmp_w1_v[l], nsa_cmp_w2_v[l], bias_c, bias_w, cover)
        y_c = _rwkv(p, rw_mu[l],
                    jnp.concatenate([rw_w2[l], zeros64], axis=0),
                    jnp.concatenate([zeros64, rw_a2[l]], axis=0),
                    rw_g2[l], rw_w0[l], rw_a0[l], rw_k_k[l], rw_k_a[l], rw_r_k[l], rw_lnx_w[l], rw_lnx_b[l])
        x = _merge(y_a, y_b, y_c, p, x, gt1, w_branch[l].astype(BF16), w_out[l].astype(BF16))
        x = _ffn(x, norm2_w[l].reshape(1, D), sh2, sc2, gt2,
                 ffn_w1[l].astype(BF16), ffn_w3[l].astype(BF16), ffn_w2[l].astype(BF16),
                 final_norm_w.reshape(1, D), l == L - 1)
    return x
```

```python
import functools
import math

import jax
import jax.numpy as jnp
import numpy as np
from jax import lax
from jax.experimental import pallas as pl
from jax.experimental.pallas import tpu as pltpu

F32 = jnp.float32
BF16 = jnp.bfloat16
I32 = jnp.int32

LANES = 128
VMEM_LIMIT = 56 * 1024 * 1024
MXU_WIDTH = 256
ROW_TILE = 1024
IN_PROJ_COL_TILE = 3 * MXU_WIDTH
FFN_COL_TILE = MXU_WIDTH
MERGE_ROW_TILE = 512

D_MODEL = 1024
A_HEADS, A_DK, A_DV = 4, 128, 128
A_CHUNK, A_SUB = 64, 16
B_HEADS, B_GROUPS, B_HD = 8, 2, 64
B_J = B_HEADS // B_GROUPS
CMP_LEN, CMP_STRIDE, CMP_HIDDEN = 32, 16, 256
SEL_LEN, SEL_TOP, N_LOCAL = 64, 16, 2
WINDOW, Q_BLOCK = 512, 128
N_WIN_TILES = WINDOW // Q_BLOCK + 1
SEL_TILES_PER_STEP = 2
REL_BUCKETS, REL_MAX_DIST = 32, 128
LOG2E = math.log2(math.e)
NSA_MASKED = -1e30
NSA_EMPTY = -1e29
C_HEADS, C_HD = 8, 64
C_CHUNK = 64
C_SUBS = 8
C_PACK = 4
C_DECAY_LORA, C_AAA_LORA, C_GATE_LORA = 64, 64, 128
LNX_EPS = 64e-5

COL_A = 0
COL_GATE = 2048 // LANES
COL_C = 5120 // LANES
COL_B = 6912 // LANES
P_COLS = 8448
P_DTYPE = BF16
NSA_Q, NSA_CMP, NSA_SEL, NSA_WIN, NSA_GATE, NSA_BLOCKS = 0, 4, 6, 8, 10, 12
NSA_GATE_STRIDE = 16


def _cparams(sem):
    return pltpu.CompilerParams(dimension_semantics=sem, vmem_limit_bytes=VMEM_LIMIT)


def _split3(x):
    hi = x.astype(BF16)
    r1 = x - hi.astype(F32)
    mid = r1.astype(BF16)
    lo = (r1 - mid.astype(F32)).astype(BF16)
    return hi, mid, lo


def _mm(a, b):
    return jnp.dot(a.astype(BF16), b.astype(BF16), preferred_element_type=F32)


def _mm_nt(a, b):
    return lax.dot_general(a.astype(BF16), b.astype(BF16), (((1,), (1,)), ((), ())),
                           preferred_element_type=F32)


def _mm_tn(a, b):
    return lax.dot_general(a.astype(BF16), b.astype(BF16), (((0,), (0,)), ((), ())),
                           preferred_element_type=F32)


def _mm_exact_lhs(a_exact, b):
    a = a_exact.astype(BF16)
    hi, mid, lo = _split3(b)
    d = functools.partial(jnp.dot, preferred_element_type=F32)
    return d(a, hi) + d(a, mid) + d(a, lo)


def _mm_exact_rhs(a, b_exact):
    b = b_exact.astype(BF16)
    hi, mid, lo = _split3(a)
    d = functools.partial(jnp.dot, preferred_element_type=F32)
    return d(hi, b) + d(mid, b) + d(lo, b)


def _mm_x3(a, b):
    ah = a.astype(BF16)
    al = (a - ah.astype(F32)).astype(BF16)
    bh = b.astype(BF16)
    bl = (b - bh.astype(F32)).astype(BF16)
    d = functools.partial(jnp.dot, preferred_element_type=F32)
    return d(ah, bh) + d(ah, bl) + d(al, bh)


def _sigmoid(x):
    return 1.0 / (1.0 + jnp.exp(-x))


def _silu(x):
    return x * _sigmoid(x)


def _log_sigmoid(x):
    return jnp.minimum(x, 0.0) - jnp.log(1.0 + jnp.exp(-jnp.abs(x)))


def _iota(shape, dim):
    return lax.broadcasted_iota(I32, shape, dim)


def _ada_kernel(c_ref, w_ref, b_ref, o_ref):
    c = c_ref[...]
    o_ref[0] = _mm_x3(_silu(c), w_ref[0]) + b_ref[0]


def _ada(c_pad, ada_w, ada_b):
    L = ada_w.shape[0]
    n_out = ada_w.shape[2]
    tn = D_MODEL
    return pl.pallas_call(
        _ada_kernel,
        grid=(L, n_out // tn),
        in_specs=[pl.BlockSpec((8, D_MODEL), lambda l, j: (0, 0)),
                  pl.BlockSpec((1, D_MODEL, tn), lambda l, j: (l, 0, j)),
                  pl.BlockSpec((1, 1, tn), lambda l, j: (l, 0, j))],
        out_specs=pl.BlockSpec((1, 8, tn), lambda l, j: (l, 0, j)),
        out_shape=jax.ShapeDtypeStruct((L, 8, n_out), F32),
        compiler_params=_cparams(("arbitrary", "arbitrary")),
        name="ada_ln",
    )(c_pad, ada_w, ada_b.reshape(L, 1, n_out))


def _norm_mod(x, nw, sh, sc):
    y = x * lax.rsqrt(jnp.mean(x * x, axis=-1, keepdims=True) + 1e-6) * nw
    return y * (1.0 + sc) + sh


def _inproj_kernel(x_ref, nw_ref, sh_ref, sc_ref, w_ref, o_ref, h_scr):
    @pl.when(pl.program_id(2) == 0)
    def _():
        h_scr[...] = _norm_mod(x_ref[0], nw_ref[...], sh_ref[0], sc_ref[0]).astype(BF16)

    o_ref[0] = jnp.dot(h_scr[...], w_ref[...], preferred_element_type=F32).astype(o_ref.dtype)


def _inproj(x, nw, sh, sc, w):
    Bsz, T, D = x.shape
    n_out = w.shape[1]
    tm, tn = min(T, ROW_TILE), IN_PROJ_COL_TILE
    return pl.pallas_call(
        _inproj_kernel,
        grid=(Bsz, T // tm, n_out // tn),
        in_specs=[pl.BlockSpec((1, tm, D), lambda b, i, n: (b, i, 0)),
                  pl.BlockSpec((1, D), lambda b, i, n: (0, 0)),
                  pl.BlockSpec((1, 1, D), lambda b, i, n: (b, 0, 0)),
                  pl.BlockSpec((1, 1, D), lambda b, i, n: (b, 0, 0)),
                  pl.BlockSpec((D, tn), lambda b, i, n: (0, n))],
        out_specs=pl.BlockSpec((1, tm, tn), lambda b, i, n: (b, i, n)),
        out_shape=jax.ShapeDtypeStruct((Bsz, T, n_out), P_DTYPE),
        scratch_shapes=[pltpu.VMEM((tm, D), BF16)],
        compiler_params=_cparams(("arbitrary", "arbitrary", "arbitrary")),
        name="in_proj",
    )(x, nw, sh, sc, w)


def _hgrn_kernel(layer, lbl_ref, nw_ref, q_ref, f_ref, i_ref, g_ref, o_ref, s_scr):
    C, c = A_CHUNK, A_SUB
    slabs = [slice(h * A_DK, (h + 1) * A_DK) for h in range(A_HEADS)]

    @pl.when(pl.program_id(1) == 0)
    def _():
        s_scr[...] = jnp.zeros_like(s_scr)

    logits = lbl_ref[...]
    ex = jnp.exp(logits - jnp.max(logits, axis=0, keepdims=True))
    sm = ex / jnp.sum(ex, axis=0, keepdims=True)
    lb = jnp.zeros((1, logits.shape[1]), F32)
    for i in range(1, layer + 1):
        lb = lb + sm[i:i + 1, :]

    q = _silu(q_ref[0].astype(F32))
    fr = f_ref[0].astype(F32)
    v = i_ref[0].astype(F32)
    la = jnp.log(lb)
    lc = jnp.log(1.0 - lb) + _log_sigmoid(fr)
    log_f = jnp.maximum(la, lc) + jnp.log(1.0 + jnp.exp(-jnp.abs(la - lc)))
    k = (1.0 - lb) * _sigmoid(-fr)

    tri = (_iota((C, C), 0) >= _iota((C, C), 1)).astype(BF16)
    b = _mm_exact_lhs(tri, log_f) * LOG2E
    b_last = b[C - 1:C, :]

    qe = q * jnp.exp2(b)
    o = [_mm_nt(qe[:, sl], s_scr[h]) for h, sl in enumerate(slabs)]
    off = [[jnp.zeros((c, A_DV), F32)] for _ in slabs]
    for blk in range(1, C // c):
        lo = blk * c
        ref = b[lo - 1:lo, :]
        qi = q[lo:lo + c] * jnp.exp2(b[lo:lo + c] - ref)
        ki = k[:lo] * jnp.exp2(ref - b[:lo])
        att = [_mm_nt(qi[:, sl], ki[:, sl]) for sl in slabs]
        for h, sl in enumerate(slabs):
            off[h].append(_mm(att[h], v[:lo, sl]))
    o = [o[h] + jnp.concatenate(off[h], axis=0) for h in range(len(slabs))]
    row = _iota((C, 1), 0) % c
    for j in range(c):
        if j == 0:
            kj, bj, vj = k, b, v
        else:
            kj, bj, vj = (pltpu.roll(t, j, 0) for t in (k, b, v))
        ok = row >= j
        term = q * kj * jnp.exp2(jnp.where(ok, b - bj, -jnp.inf))
        for h, sl in enumerate(slabs):
            o[h] = o[h] + jnp.sum(term[:, sl], axis=1, keepdims=True) * vj[:, sl]

    kd = k * jnp.exp2(b_last - b)
    p_end = jnp.exp2(b_last)
    for h, sl in enumerate(slabs):
        s_scr[h] = s_scr[h] * p_end[:, sl] + _mm_tn(v[:, sl], kd[:, sl])

    gate = _sigmoid(g_ref[0].astype(F32))
    for h, sl in enumerate(slabs):
        oh = o[h]
        oh = oh * lax.rsqrt(jnp.mean(oh * oh, axis=-1, keepdims=True) + 1e-5) * nw_ref[:, sl]
        o_ref[0, :, sl] = oh * gate[:, sl]


def _hgrn(p, lb_logits, norm_w, layer):
    Bsz, T, _ = p.shape
    C = A_CHUNK
    L = lb_logits.shape[0]
    W = A_HEADS * A_DK
    cw = COL_A * LANES // W

    def col(i):
        return pl.BlockSpec((1, C, W), lambda b, c: (b, c, cw + i))

    return pl.pallas_call(
        functools.partial(_hgrn_kernel, layer),
        grid=(Bsz, T // C),
        in_specs=[pl.BlockSpec((L, W), lambda b, c: (0, 0)),
                  pl.BlockSpec((1, A_HEADS * A_DV), lambda b, c: (0, 0)),
                  col(0), col(1), col(2), col(3)],
        out_specs=pl.BlockSpec((1, C, A_HEADS * A_DV), lambda b, c: (b, c, 0)),
        out_shape=jax.ShapeDtypeStruct((Bsz, T, A_HEADS * A_DV), F32),
        scratch_shapes=[pltpu.VMEM((A_HEADS, A_DV, A_DK), F32)],
        compiler_params=_cparams(("arbitrary", "arbitrary")),
        name="hgrn2",
    )(lb_logits, norm_w.reshape(1, -1), p, p, p, p)


def _rwkv_kernel(r_ref, k_ref, v_ref, wa_ref, xg_ref, mu_ref,
                 w2_ref, a2_ref, g2_ref, w0_ref, a0_ref, kk_ref, ka_ref, rk_ref, lnw_ref, lnb_ref,
                 o_ref, s_scr, pr_scr, pk_scr, pv_scr, pwa_scr, pg_scr):
    C = C_CHUNK
    W = C_HEADS * C_HD
    n_hp = W // LANES

    @pl.when(pl.program_id(1) == 0)
    def _():
        s_scr[...] = jnp.zeros_like(s_scr)
        for scr in (pr_scr, pk_scr, pv_scr, pwa_scr, pg_scr):
            scr[...] = jnp.zeros_like(scr)

    R = C_SUBS * C
    subs = [slice(s * C, (s + 1) * C) for s in range(C_SUBS)]
    row0 = _iota((R, 1), 0) == 0

    def shifted(ref, prev_scr, mu):
        cur = ref[0].astype(F32)
        prev = jnp.where(row0, prev_scr[...], pltpu.roll(cur, 1, 0))
        prev_scr[...] = cur[R - 1:R, :]
        return cur + (prev - cur) * mu

    r = shifted(r_ref, pr_scr, mu_ref[:, 0:W])
    k = shifted(k_ref, pk_scr, mu_ref[:, W:2 * W])
    v = shifted(v_ref, pv_scr, mu_ref[:, 2 * W:3 * W])
    xwa = shifted(wa_ref, pwa_scr, mu_ref[:, 3 * W:3 * W + LANES])
    xg = shifted(xg_ref, pg_scr, mu_ref[:, 3 * W + LANES:])

    w = w0_ref[...] + _mm_x3(jnp.tanh(xwa), w2_ref[...])
    sp = jnp.maximum(-w, 0.0) + jnp.log(1.0 + jnp.exp(-jnp.abs(w)))
    lw = -jnp.exp(-sp - 0.5)
    a = _sigmoid(a0_ref[...] + _mm_x3(xwa, a2_ref[...]))
    g = _mm_x3(_sigmoid(xg), g2_ref[...])

    ri, rj = _iota((R, R), 0), _iota((R, R), 1)
    tri_incl = ((ri >= rj) & (ri // C == rj // C)).astype(BF16)
    b = _mm_exact_lhs(tri_incl, lw)
    b_last = [b[rs.stop - 1:rs.stop, :] for rs in subs]
    p_end = [jnp.exp(bl) for bl in b_last]
    inv_p = jnp.exp(-b)
    e_end = jnp.concatenate([jnp.exp(bl - b[rs]) for bl, rs in zip(b_last, subs)], axis=0)
    kk_raw = k * kk_ref[...]
    k = k * (1.0 + (a - 1.0) * ka_ref[...])
    bonus_in = r * k * rk_ref[...]
    r_t = r * jnp.exp(b)
    p_prev = jnp.exp(b - lw)

    bd_f = ((_iota((LANES, LANES), 0) < C_HD) == (_iota((LANES, LANES), 1) < C_HD)).astype(F32)
    slabs = [slice(hp * LANES, (hp + 1) * LANES) for hp in range(n_hp)]
    dot = functools.partial(jnp.dot, preferred_element_type=F32)
    bd_b = bd_f.astype(BF16)

    PH = C_PACK
    PW = PH * C_HD
    n_packs = W // PW
    packs = [slice(i * PW, (i + 1) * PW) for i in range(n_packs)]
    s_lane = _iota((C, PH * C), 1) % C
    t_row = _iota((C, PH * C), 0)
    lower_strict = t_row > s_lane
    lower_incl = t_row >= s_lane
    same_head = (_iota((PH * C, PW), 0) // C) == (_iota((PH * C, PW), 1) // C_HD)
    state_mask = (_iota((PW, PW), 0) // C_HD) == (_iota((PW, PW), 1) // C_HD)

    def spread(x):
        xb = x.astype(BF16)
        return jnp.where(same_head, jnp.concatenate([xb] * PH, axis=0), jnp.zeros((), BF16))

    def group_sums(x):
        st = jnp.concatenate([x[:, sl] for sl in slabs], axis=0)
        hi = st.astype(BF16)
        lo = (st - hi.astype(F32)).astype(BF16)
        out = dot(hi, bd_b) + dot(lo, bd_b)
        return jnp.concatenate([out[i * R:(i + 1) * R] for i in range(n_hp)], axis=1)

    kk = kk_raw * lax.rsqrt(jnp.maximum(group_sums(kk_raw * kk_raw), 1e-24))
    al = a * kk
    bonus = group_sums(bonus_in) * v
    kap = kk * p_prev
    al_h = al * inv_p
    k_h = k * inv_p
    al_e = al * e_end
    k_e = k * e_end

    PC = PH * C
    units = [(rs, pk) for rs in subs for pk in packs]
    lhs = [jnp.concatenate([kap[rs, pk], r_t[rs, pk]], axis=0).astype(BF16) for rs, pk in units]
    e_st = [jnp.concatenate([al_e[rs, pk], k_e[rs, pk]], axis=0).astype(BF16) for rs, pk in units]
    v_b = [v[rs, pk].astype(BF16) for rs, pk in units]
    v_sp = [spread(v[rs, pk]) for rs, pk in units]
    big = [_mm_nt(lhs[i], jnp.concatenate([spread(al_h[rs, pk]), spread(k_h[rs, pk])], axis=0))
           for i, (rs, pk) in enumerate(units)]
    a_ak = [jnp.where(lower_strict, m[:C, :PC], 0.0) for m in big]
    b_al = [jnp.where(lower_incl, m[C:, :PC], 0.0) for m in big]
    kv = [dot(jnp.concatenate([jnp.where(lower_strict, m[:C, PC:], 0.0),
                               jnp.where(lower_incl, m[C:, PC:], 0.0)], axis=0).astype(BF16), v_sp[i])
          for i, m in enumerate(big)]

    eye = (t_row == s_lane).astype(F32)
    n = [-m for m in a_ak]
    x = [eye + m for m in n]
    n = [dot(m.astype(BF16), spread(m)) for m in n]
    power = 2
    while 2 * power < C:
        for i in range(len(units)):
            res = dot(jnp.concatenate([x[i], n[i]], axis=0).astype(BF16), spread(n[i]))
            x[i] = x[i] + res[:C]
            n[i] = res[C:]
        power *= 2
    x = [x[i] + dot(x[i].astype(BF16), spread(n[i])) for i in range(len(units))]

    state = [s_scr[i] for i in range(n_packs)]
    y_rows = []
    for si in range(C_SUBS):
        y_parts = []
        for pi, pk in enumerate(packs):
            i = si * n_packs + pi
            ks = _mm_nt(lhs[i], state[pi])
            u = -dot(x[i].astype(BF16), spread(ks[:C] + kv[i][:C]))
            y_parts.append(ks[C:] + dot(b_al[i].astype(BF16), spread(u)) + kv[i][C:])
            upd = _mm_tn(jnp.concatenate([u.astype(BF16), v_b[i]], axis=0), e_st[i])
            state[pi] = state[pi] * p_end[si][:, pk] + jnp.where(state_mask, upd, 0.0)
        y_rows.append(jnp.concatenate(y_parts, axis=1))
    for pi in range(n_packs):
        s_scr[pi] = state[pi]

    y = jnp.concatenate(y_rows, axis=0)
    inv_hd = 1.0 / C_HD
    yc = y - group_sums(y) * inv_hd
    var = group_sums(yc * yc) * inv_hd
    y = yc * lax.rsqrt(var + LNX_EPS) * lnw_ref[...] + lnb_ref[...]
    o_ref[0] = (y + bonus) * g


def _rwkv(p, mu, w2p, a2p, g2, w0, a0, k_k, k_a, r_k, lnx_w, lnx_b):
    Bsz, T, _ = p.shape
    C = C_SUBS * C_CHUNK
    W = C_HEADS * C_HD
    n_hp = W // LANES
    cw = COL_C * LANES // W
    pw = C_PACK * C_HD
    assert C_CHUNK == C_HD and W % pw == 0

    def wide(i):
        return pl.BlockSpec((1, C, W), lambda b, c: (b, c, cw + i))

    def narrow(i):
        return pl.BlockSpec((1, C, LANES), lambda b, c: (b, c, COL_C + 3 * n_hp + i))

    def const(shape):
        return pl.BlockSpec(shape, lambda b, c: (0,) * len(shape))

    row = lambda t: t.reshape(1, -1)
    vecs = [row(t) for t in (w0, a0, k_k, k_a, r_k, lnx_w, lnx_b)]
    return pl.pallas_call(
        _rwkv_kernel,
        grid=(Bsz, T // C),
        in_specs=[wide(0), wide(1), wide(2), narrow(0), narrow(1), const((1, mu.shape[0])),
                  const(w2p.shape), const(a2p.shape), const(g2.shape)] + [const((1, W))] * 7,
        out_specs=pl.BlockSpec((1, C, W), lambda b, c: (b, c, 0)),
        out_shape=jax.ShapeDtypeStruct((Bsz, T, W), F32),
        scratch_shapes=[pltpu.VMEM((W // pw, pw, pw), F32)] + [pltpu.VMEM((1, W), F32)] * 3
                       + [pltpu.VMEM((1, LANES), F32)] * 2,
        compiler_params=_cparams(("arbitrary", "arbitrary")),
        name="rwkv7",
    )(p, p, p, p, p, row(mu), w2p, a2p, g2, *vecs)


def _t5_thresholds():
    max_exact = REL_BUCKETS // 2
    out = []
    for kk in range(1, REL_BUCKETS - max_exact):
        n = max_exact
        while int(math.log(n / max_exact) / math.log(REL_MAX_DIST / max_exact) * (REL_BUCKETS - max_exact)) < kk:
            n += 1
        out.append(n)
    return out


def _bias_from_dist(dist, rb_ref, head, valid):
    n = jnp.maximum(dist, 0)
    max_exact = REL_BUCKETS // 2
    large = jnp.full(n.shape, max_exact, I32)
    for thr in _t5_thresholds():
        large = large + (n >= thr).astype(I32)
    bucket = jnp.where(n < max_exact, n, large)
    out = jnp.zeros(n.shape, F32)
    for j in range(REL_BUCKETS):
        out = jnp.where(bucket == j, rb_ref[j, head] * LOG2E, out)
    return jnp.where(valid, out, NSA_MASKED)


def _bias_cmp_kernel(rb_ref, o_ref):
    g, qb = pl.program_id(0), pl.program_id(1)
    n_pad = o_ref.shape[1]
    t = qb * Q_BLOCK + _iota((n_pad, Q_BLOCK), 1)
    n = _iota((n_pad, Q_BLOCK), 0)
    dist = t - (n * CMP_STRIDE + CMP_LEN - 1)
    valid = (dist >= 0) & (n < n_pad - 1)
    for j in range(B_J):
        o_ref[0, :, j * Q_BLOCK:(j + 1) * Q_BLOCK] = _bias_from_dist(dist, rb_ref, g * B_J + j, valid)


def _bias_win_kernel(rb_ref, o_ref):
    g, d = pl.program_id(0), pl.program_id(1)
    dist = d * Q_BLOCK + _iota((Q_BLOCK, Q_BLOCK), 1) - _iota((Q_BLOCK, Q_BLOCK), 0)
    valid = (dist >= 0) & (dist < WINDOW) & (d < N_WIN_TILES)
    for j in range(B_J):
        o_ref[0, 0, :, j * Q_BLOCK:(j + 1) * Q_BLOCK] = _bias_from_dist(dist, rb_ref, g * B_J + j, valid)


def _bias_tables(rel_bias, T, n_pad):
    smem = pl.BlockSpec(memory_space=pltpu.SMEM)
    jq = B_J * Q_BLOCK
    n_q = T // Q_BLOCK
    bias_c = pl.pallas_call(
        _bias_cmp_kernel,
        grid=(B_GROUPS, n_q),
        in_specs=[smem],
        out_specs=pl.BlockSpec((1, n_pad, jq), lambda g, i: (g, 0, i)),
        out_shape=jax.ShapeDtypeStruct((B_GROUPS, n_pad, n_q * jq), F32),
        compiler_params=_cparams(("arbitrary", "arbitrary")),
        name="nsa_bias_cmp",
    )(rel_bias)
    bias_w = pl.pallas_call(
        _bias_win_kernel,
        grid=(B_GROUPS, N_WIN_TILES + 1),
        in_specs=[smem],
        out_specs=pl.BlockSpec((1, 1, Q_BLOCK, jq), lambda g, d: (g, d, 0, 0)),
        out_shape=jax.ShapeDtypeStruct((B_GROUPS, N_WIN_TILES + 1, Q_BLOCK, jq), F32),
        compiler_params=_cparams(("arbitrary", "arbitrary")),
        name="nsa_bias_win",
    )(rel_bias)
    return bias_c, bias_w


def _compress_kernel(x_ref, pe_ref, w1_ref, w2_ref, o_ref, xs_scr):
    n_grp = o_ref.shape[2]
    hidden = w1_ref.shape[2]
    ha = jnp.zeros((n_grp, hidden), F32)
    hb = jnp.zeros((n_grp, hidden), F32)
    xs_scr[...] = x_ref[0].astype(F32)
    for tau in range(CMP_STRIDE):
        x_tau = xs_scr[pl.ds(tau, n_grp, stride=CMP_STRIDE), :]
        ha = ha + _mm(x_tau + pe_ref[tau:tau + 1, :], w1_ref[tau])
        hb = hb + _mm(x_tau + pe_ref[CMP_STRIDE + tau:CMP_STRIDE + tau + 1, :], w1_ref[CMP_STRIDE + tau])
    h = ha + pltpu.roll(hb, n_grp - 1, 0)
    out = _mm(_silu(h), w2_ref[...])
    o_ref[0, 0] = jnp.where(_iota(out.shape, 0) < n_grp - 1, out, 0.0)


def _compress(p, pe, w1, w2):
    Bsz, T, _ = p.shape
    n_grp = T // CMP_STRIDE
    return pl.pallas_call(
        _compress_kernel,
        grid=(Bsz, B_GROUPS),
        in_specs=[pl.BlockSpec((1, T, LANES), lambda b, g: (b, 0, COL_B + NSA_CMP + g)),
                  pl.BlockSpec(pe.shape, lambda b, g: (0, 0)),
                  pl.BlockSpec(w1.shape, lambda b, g: (0, 0, 0)),
                  pl.BlockSpec(w2.shape, lambda b, g: (0, 0))],
        out_specs=pl.BlockSpec((1, 1, n_grp, LANES), lambda b, g: (b, g, 0, 0)),
        out_shape=jax.ShapeDtypeStruct((Bsz, B_GROUPS, n_grp, LANES), F32),
        scratch_shapes=[pltpu.VMEM((T, LANES), F32)],
        compiler_params=_cparams(("arbitrary", "arbitrary")),
        name="nsa_compress",
    )(p, pe, w1, w2)


def _nsa_kernel(q_ref, gate_ref, kvc_ref, ks_ref, kw_ref, bc_ref, bw_ref, cov_ref, o_ref,
                ks_scr, kw_scr, vs_scr, vw_scr, gate_scr, sc0_scr, sc1_scr, vs2_scr, qf_scr):
    QB, J, HD = Q_BLOCK, B_J, B_HD
    JQ = J * QB
    grp = pl.program_id(1)
    qb = pl.program_id(2)
    t0 = qb * QB
    n_tiles = ks_ref.shape[1] // QB
    NEG = -jnp.inf
    MASKED, EMPTY = NSA_MASKED, NSA_EMPTY
    dot = functools.partial(jnp.dot, preferred_element_type=F32)

    @pl.when(qb == 0)
    def _():
        lane = _iota((1, LANES), 1)
        key_blk = _iota((QB, 1), 0) // SEL_LEN

        def fill(pair, carry):
            for half in range(2):
                kt = 2 * pair + half
                rows = pl.ds(pl.multiple_of(kt * QB, QB), QB)
                tile = ks_ref[0, rows, :].astype(F32)
                k_ones = jnp.where(lane < HD, tile, jnp.where(lane < HD + 2, 1.0, 0.0))
                one_hot = (lane == key_blk + kt * (QB // SEL_LEN)).astype(F32)
                ks_scr[rows, :] = jnp.concatenate([k_ones, one_hot], axis=1).astype(BF16)
                v_t = tile.T[HD:, :].astype(BF16)
                vs_scr[kt] = v_t
                vs2_scr[pair, :, half * QB:(half + 1) * QB] = v_t
                tile = kw_ref[0, rows, :].astype(F32)
                kw_scr[rows, :] = tile.astype(BF16)
                vw_scr[kt] = tile.T[HD:, :].astype(BF16)
            return carry
        lax.fori_loop(0, n_tiles // 2, fill, 0)

    q_rows = (q_ref[0].astype(F32) * (HD ** -0.5 * LOG2E)).T
    q_t = jnp.concatenate([q_rows[j * HD:(j + 1) * HD] for j in range(J)], axis=1)
    q_t = jnp.concatenate([q_t, jnp.zeros_like(q_t)], axis=0).astype(BF16)
    t_q = t0 + _iota((1, QB), 1)

    def finish(e_sum):
        return 1.0 / jnp.maximum(e_sum, 1e-30)

    def softmax_parts(scores):
        m = functools.reduce(jnp.maximum, [jnp.max(sc, axis=0, keepdims=True) for sc in scores])
        m = jnp.where(m < EMPTY, 0.0, m)
        es = [jnp.exp2(sc - m) for sc in scores]
        tot = functools.reduce(jnp.add, [jnp.sum(e, axis=0, keepdims=True) for e in es])
        return es, finish(tot)

    kvc = kvc_ref[0, 0].astype(BF16)
    (e_c,), inv_c = softmax_parts([dot(kvc, q_t) + bc_ref[0]])
    p_c = e_c * inv_c
    o_c = _mm_tn(kvc, p_c)[HD:]

    p_sum = p_c[:, 0:QB]
    for j in range(1, J):
        p_sum = p_sum + p_c[:, j * QB:(j + 1) * QB]
    ph = p_sum.astype(BF16)
    pl_ = (p_sum - ph.astype(F32)).astype(BF16)
    cov = cov_ref[...]
    imp = dot(cov, ph) + dot(cov, pl_)
    n_blk = cov.shape[0]
    m_col = _iota((n_blk, 1), 0)
    cur = t_q // SEL_LEN
    causal = m_col <= cur
    forced = (m_col == 0) | ((m_col > cur - N_LOCAL) & causal)
    score = jnp.where(causal, jnp.where(forced, jnp.inf, imp), NEG)
    n_real = ks_ref.shape[1] // SEL_LEN
    st = score[:n_real, :]
    SUB = 8
    row_in_group = _iota((SUB, 1), 0)
    groups = [st[lo:lo + SUB, :] for lo in range(0, n_real, SUB)]
    cnts = [jnp.zeros((SUB, QB), F32) for _ in groups]
    for mp in range(n_real):
        other = st[mp:mp + 1, :]
        for gi, rows in enumerate(groups):
            lo = gi * SUB
            if lo > mp:
                beats = other >= rows
            elif lo + SUB - 1 < mp:
                beats = other > rows
            else:
                beats = (other > rows) | ((other == rows) & (row_in_group > mp - lo))
            cnts[gi] = cnts[gi] + beats.astype(F32)
    selected = jnp.concatenate(cnts, axis=0) < float(SEL_TOP)

    q_rows = q_t[:HD].astype(F32)
    mask_rows = jnp.where(selected, 0.0, MASKED)
    if n_real < n_blk:
        mask_rows = jnp.concatenate([mask_rows, jnp.full((n_blk - n_real, QB), MASKED, F32)], axis=0)
    mask_rows = jnp.concatenate([mask_rows] * J, axis=1)
    far_bias = bw_ref[0, 2, 0:1, :]
    fb_hi = far_bias.astype(BF16).astype(F32)
    row = _iota((HD, 1), 0)
    bias_rows = jnp.where(row == 0, fb_hi, jnp.where(row == 1, far_bias - fb_hi, 0.0))
    qf_scr[0] = jnp.concatenate([q_rows, bias_rows, mask_rows], axis=0).astype(BF16)
    qf_scr[1] = jnp.concatenate([jnp.zeros((2 * HD, JQ), F32), jnp.full((n_blk, JQ), MASKED, F32)],
                                axis=0).astype(BF16)
    q_near = jnp.concatenate([q_rows, jnp.zeros_like(bias_rows), mask_rows], axis=0).astype(BF16)

    def key_tile(scr, kt):
        return scr[pl.ds(pl.multiple_of(kt * QB, QB), QB), :]

    def online(carry, scores, pv):
        m_old, l_old, acc = carry
        m_new = functools.reduce(jnp.maximum, [m_old] + [jnp.max(sc, axis=0, keepdims=True) for sc in scores])
        m_use = jnp.where(m_new < EMPTY, 0.0, m_new)
        alpha = jnp.exp2(m_old - m_use)
        prs = [jnp.exp2(sc - m_use) for sc in scores]
        l_new = alpha * l_old + functools.reduce(jnp.add, [jnp.sum(pr, axis=0, keepdims=True) for pr in prs])
        return m_new, l_new, alpha * acc + pv([pr.astype(BF16) for pr in prs])

    S = SEL_TILES_PER_STEP
    n_far = jnp.maximum(qb - 1, 0)

    def stage_scores(group, buf):
        for u in range(S):
            kt = group * S + u
            rhs = qf_scr[jnp.where(kt < n_far, 0, 1)]
            buf[u * QB:(u + 1) * QB, :] = dot(key_tile(ks_scr, jnp.minimum(kt, n_tiles - 1)), rhs)

    def process(group, buf, carry):
        for h in range(S // 2):
            v_pair = vs2_scr[jnp.minimum(group * (S // 2) + h, n_tiles // 2 - 1)]
            carry = online(carry, [buf[(2 * h + u) * QB:(2 * h + u + 1) * QB, :] for u in range(2)],
                           lambda prs, v_pair=v_pair: dot(v_pair, jnp.concatenate(prs, axis=0)))
        return carry

    def far_step(it, carry):
        group = 2 * it
        stage_scores(group + 1, sc1_scr)
        carry = process(group, sc0_scr, carry)
        stage_scores(group + 2, sc0_scr)
        return process(group + 1, sc1_scr, carry)

    state = (jnp.full((1, JQ), NEG, F32), jnp.zeros((1, JQ), F32), jnp.zeros((HD, JQ), F32))
    stage_scores(0, sc0_scr)
    n_groups = (n_far + S - 1) // S
    state = lax.fori_loop(0, (n_groups + 1) // 2, far_step, state)
    kt_prev = jnp.maximum(qb - 1, 0)
    s_prev = dot(key_tile(ks_scr, kt_prev), q_near) + bw_ref[0, jnp.where(qb >= 1, 1, N_WIN_TILES)]
    s_diag = dot(key_tile(ks_scr, qb), q_near) + bw_ref[0, 0]
    _, l_s, acc_s = online(state, [s_prev, s_diag],
                           lambda prs: dot(vs_scr[kt_prev], prs[0]) + dot(vs_scr[qb], prs[1]))
    o_s = acc_s * finish(l_s)

    scores, values = [], []
    for d in range(N_WIN_TILES):
        kt = jnp.maximum(qb - d, 0)
        start = pl.multiple_of(kt * QB, QB)
        scores.append(dot(kw_scr[pl.ds(start, QB), :], q_t) + bw_ref[0, jnp.where(qb >= d, d, N_WIN_TILES)])
        values.append(vw_scr[kt])
    es, inv_w = softmax_parts(scores)
    acc_w = functools.reduce(jnp.add, [dot(v_t, e.astype(BF16)) for v_t, e in zip(values, es)])
    o_w = acc_w * inv_w

    gate_scr[...] = _sigmoid(gate_ref[0].astype(F32)).T
    g_rows = gate_scr[pl.ds(pl.multiple_of(grp * NSA_GATE_STRIDE, NSA_GATE_STRIDE), NSA_GATE_STRIDE), :]
    g_c, g_s, g_w = (jnp.concatenate([g_rows[r * J + j:r * J + j + 1, :] for j in range(J)], axis=1)
                     for r in range(3))
    o_t = g_c * o_c + g_s * o_s + g_w * o_w
    o_ref[0] = jnp.concatenate([o_t[:, j * QB:(j + 1) * QB] for j in range(J)], axis=0).T


def _nsa(p, kvc, bias_c, bias_w, cover_t):
    Bsz, T, _ = p.shape
    G, J, HD, QB = B_GROUPS, B_J, B_HD, Q_BLOCK
    n_q = T // QB
    n_pad = kvc.shape[2]
    JQ = J * QB
    q_blk = COL_B * LANES // (J * HD)

    def kv_rows(first):
        return pl.BlockSpec((1, T, LANES), lambda b, g, i: (b, 0, COL_B + first + g))

    return pl.pallas_call(
        _nsa_kernel,
        grid=(Bsz, G, n_q),
        in_specs=[pl.BlockSpec((1, QB, J * HD), lambda b, g, i: (b, i, q_blk + g)),
                  pl.BlockSpec((1, QB, LANES), lambda b, g, i: (b, i, COL_B + NSA_GATE)),
                  pl.BlockSpec((1, 1, n_pad, LANES), lambda b, g, i: (b, g, 0, 0)),
                  kv_rows(NSA_SEL), kv_rows(NSA_WIN),
                  pl.BlockSpec((1, n_pad, JQ), lambda b, g, i: (g, 0, i)),
                  pl.BlockSpec((1, N_WIN_TILES + 1, QB, JQ), lambda b, g, i: (g, 0, 0, 0)),
                  pl.BlockSpec(cover_t.shape, lambda b, g, i: (0, 0))],
        out_specs=pl.BlockSpec((1, QB, J * HD), lambda b, g, i: (b, i, g)),
        out_shape=jax.ShapeDtypeStruct((Bsz, T, G * J * HD), F32),
        scratch_shapes=[pltpu.VMEM((T, 2 * LANES), BF16), pltpu.VMEM((T, LANES), BF16),
                        pltpu.VMEM((n_q, HD, QB), BF16), pltpu.VMEM((n_q, HD, QB), BF16),
                        pltpu.VMEM((LANES, QB), F32),
                        pltpu.VMEM((SEL_TILES_PER_STEP * QB, JQ), F32),
                        pltpu.VMEM((SEL_TILES_PER_STEP * QB, JQ), F32),
                        pltpu.VMEM((n_q // 2, HD, 2 * QB), BF16),
                        pltpu.VMEM((2, 2 * HD + LANES, JQ), BF16)],
        compiler_params=_cparams(("arbitrary", "arbitrary", "arbitrary")),
        name="nsa_attention",
    )(p, p, kvc, p, p, bias_c, bias_w, cover_t)


def _merge_kernel(ya_ref, yb_ref, yc_ref, ga_ref, gb_ref, gc_ref, x_ref, gt_ref, wb_ref, wo_ref, o_ref):
    merged = (_sigmoid(ga_ref[0].astype(F32)) * _mm(ya_ref[0], wb_ref[0])
              + _sigmoid(gb_ref[0].astype(F32)) * _mm(yb_ref[0], wb_ref[1])
              + _sigmoid(gc_ref[0].astype(F32)) * _mm(yc_ref[0], wb_ref[2]))
    o_ref[0] = x_ref[0] + gt_ref[0] * _mm(merged, wo_ref[...])


def _merge(ya, yb, yc, p, x, gt, wb, wo):
    Bsz, T, D = x.shape
    tm = MERGE_ROW_TILE
    bw = ya.shape[2]
    gate_blk = COL_GATE * LANES // D
    yspec = pl.BlockSpec((1, tm, bw), lambda b, i: (b, i, 0))

    def gspec(k):
        return pl.BlockSpec((1, tm, D), lambda b, i: (b, i, gate_blk + k))

    return pl.pallas_call(
        _merge_kernel,
        grid=(Bsz, T // tm),
        in_specs=[yspec, yspec, yspec, gspec(0), gspec(1), gspec(2),
                  pl.BlockSpec((1, tm, D), lambda b, i: (b, i, 0)),
                  pl.BlockSpec((1, 1, D), lambda b, i: (b, 0, 0)),
                  pl.BlockSpec(wb.shape, lambda b, i: (0, 0, 0)),
                  pl.BlockSpec(wo.shape, lambda b, i: (0, 0))],
        out_specs=pl.BlockSpec((1, tm, D), lambda b, i: (b, i, 0)),
        out_shape=jax.ShapeDtypeStruct(x.shape, F32),
        compiler_params=_cparams(("arbitrary", "arbitrary")),
        name="merge_out_proj",
    )(ya, yb, yc, p, p, p, x, gt, wb, wo)


def _ffn_kernel(final, x_ref, nw_ref, sh_ref, sc_ref, gt_ref, w1_ref, w3_ref, w2_ref, fw_ref, o_ref,
                h_scr, acc_scr):
    f = pl.program_id(2)

    @pl.when(f == 0)
    def _():
        h_scr[...] = _norm_mod(x_ref[0], nw_ref[...], sh_ref[0], sc_ref[0]).astype(BF16)
        acc_scr[...] = jnp.zeros_like(acc_scr)

    h = h_scr[...]
    u = _silu(jnp.dot(h, w1_ref[...], preferred_element_type=F32)) * jnp.dot(h, w3_ref[...],
                                                                           preferred_element_type=F32)
    acc_scr[...] += _mm(u, w2_ref[...])

    @pl.when(f == pl.num_programs(2) - 1)
    def _():
        x = x_ref[0] + gt_ref[0] * acc_scr[...]
        if final:
            x = x * lax.rsqrt(jnp.mean(x * x, axis=-1, keepdims=True) + 1e-6) * fw_ref[...]
        o_ref[0] = x


def _ffn(x, nw, sh, sc, gt, w1, w3, w2, fw, final):
    Bsz, T, D = x.shape
    tm, tf = min(T, ROW_TILE), FFN_COL_TILE
    n_ff = w1.shape[1]
    vec = pl.BlockSpec((1, D), lambda b, i, f: (0, 0))
    bvec = pl.BlockSpec((1, 1, D), lambda b, i, f: (b, 0, 0))
    xspec = pl.BlockSpec((1, tm, D), lambda b, i, f: (b, i, 0))
    return pl.pallas_call(
        functools.partial(_ffn_kernel, final),
        grid=(Bsz, T // tm, n_ff // tf),
        in_specs=[xspec, vec, bvec, bvec, bvec,
                  pl.BlockSpec((D, tf), lambda b, i, f: (0, f)),
                  pl.BlockSpec((D, tf), lambda b, i, f: (0, f)),
                  pl.BlockSpec((tf, D), lambda b, i, f: (f, 0)),
                  vec],
        out_specs=xspec,
        out_shape=jax.ShapeDtypeStruct(x.shape, F32),
        scratch_shapes=[pltpu.VMEM((tm, D), BF16), pltpu.VMEM((tm, D), F32)],
        compiler_params=_cparams(("arbitrary", "arbitrary", "arbitrary")),
        name="ffn_swiglu",
    )(x, nw, sh, sc, gt, w1, w3, w2, fw)


def _pack_w_in(w):
    a_in, b_in, c_in = 2048, 1304, 1792
    wa = w[:, :a_in]
    wb = w[:, a_in:a_in + b_in]
    wc = w[:, a_in + b_in:a_in + b_in + c_in]
    wg = w[:, a_in + b_in + c_in:]
    src, used = _nsa_columns()
    wb = jnp.where(used[None, :], jnp.take(wb, src, axis=1), 0.0)
    return jnp.concatenate([wa, wg, wc, wb], axis=1).astype(BF16)


def _nsa_columns():
    G, J, HD = B_GROUPS, B_J, B_HD
    width = NSA_BLOCKS * LANES
    src = np.zeros((width,), np.int32)
    used = np.zeros((width,), bool)
    n_q = G * J * HD
    src[:n_q] = np.arange(n_q)
    used[:n_q] = True
    for pair, blk in enumerate((NSA_CMP, NSA_SEL, NSA_WIN)):
        for g in range(G):
            for kv in range(2):
                dst = (blk + g) * LANES + kv * HD
                src[dst:dst + HD] = n_q + (2 * pair + kv) * G * HD + g * HD + np.arange(HD)
                used[dst:dst + HD] = True
    gate0 = n_q + 6 * G * HD
    for g in range(G):
        for j in range(J):
            for r in range(3):
                dst = NSA_GATE * LANES + g * NSA_GATE_STRIDE + r * J + j
                src[dst] = gate0 + (g * J + j) * 3 + r
                used[dst] = True
    return src, used


def _nsa_layer(p, pe_k, w1_k, w2_k, pe_v, w1_v, w2_v, bias_c, bias_w, cover):
    HD = B_HD
    w1k = w1_k.reshape(CMP_LEN, HD, CMP_HIDDEN)
    w1v = w1_v.reshape(CMP_LEN, HD, CMP_HIDDEN)
    zero1 = jnp.zeros_like(w1k)
    w1 = jnp.concatenate([jnp.concatenate([w1k, zero1], axis=2),
                          jnp.concatenate([zero1, w1v], axis=2)], axis=1).astype(BF16)
    zero2 = jnp.zeros_like(w2_k)
    w2 = jnp.concatenate([jnp.concatenate([w2_k, zero2], axis=1),
                          jnp.concatenate([zero2, w2_v], axis=1)], axis=0).astype(BF16)
    pe = jnp.concatenate([pe_k, pe_v], axis=1)
    kvc = _compress(p, pe, w1, w2)
    return _nsa(p, kvc, bias_c, bias_w, cover)


def _cover_matrix(T, n_pad):
    n = jnp.arange(n_pad)[None, :] * CMP_STRIDE
    m = jnp.arange(LANES)[:, None] * SEL_LEN
    real = (jnp.arange(n_pad)[None, :] < (T - CMP_LEN) // CMP_STRIDE + 1) & (jnp.arange(LANES)[:, None] < T // SEL_LEN)
    return ((n < m + SEL_LEN) & (n + CMP_LEN > m) & real).astype(BF16)


def kernel(x, c, ada_w, ada_b, norm1_w, norm2_w, w_in, hgrn_lb_logits, hgrn_norm_w, nsa_pe_k, nsa_cmp_w1_k, nsa_cmp_w2_k, nsa_pe_v, nsa_cmp_w1_v, nsa_cmp_w2_v, rel_bias, rw_mu, rw_w0, rw_w2, rw_a0, rw_a2, rw_g2, rw_k_k, rw_k_a, rw_r_k, rw_lnx_w, rw_lnx_b, w_branch, w_out, ffn_w1, ffn_w3, ffn_w2, final_norm_w):
    Bsz, T, D = x.shape
    L = ada_w.shape[0]
    n_pad = T // CMP_STRIDE
    c_pad = jnp.pad(c, ((0, 8 - Bsz), (0, 0)))
    ada = _ada(c_pad, ada_w, ada_b)[:, :Bsz]
    bias_c, bias_w = _bias_tables(rel_bias, T, n_pad)
    cover = _cover_matrix(T, n_pad)
    zeros64 = jnp.zeros((C_DECAY_LORA, C_HEADS * C_HD), F32)
    for l in range(L):
        sh1, sc1, gt1, sh2, sc2, gt2 = (ada[l, :, None, i * D:(i + 1) * D] for i in range(6))
        p = _inproj(x, norm1_w[l].reshape(1, D), sh1, sc1, _pack_w_in(w_in[l]))
        y_a = _hgrn(p, hgrn_lb_logits, hgrn_norm_w[l], l)
        y_b = _nsa_layer(p, nsa_pe_k[l], nsa_cmp_w1_k[l], nsa_cmp_w2_k[l],
                         nsa_pe_v[l], nsa_cmp_w1_v[l], nsa_cmp_w2_v[l], bias_c, bias_w, cover)
        y_c = _rwkv(p, rw_mu[l],
                    jnp.concatenate([rw_w2[l], zeros64], axis=0),
                    jnp.concatenate([zeros64, rw_a2[l]], axis=0),
                    rw_g2[l], rw_w0[l], rw_a0[l], rw_k_k[l], rw_k_a[l], rw_r_k[l], rw_lnx_w[l], rw_lnx_b[l])
        x = _merge(y_a, y_b, y_c, p, x, gt1, w_branch[l].astype(BF16), w_out[l].astype(BF16))
        x = _ffn(x, norm2_w[l].reshape(1, D), sh2, sc2, gt2,
                 ffn_w1[l].astype(BF16), ffn_w3[l].astype(BF16), ffn_w2[l].astype(BF16),
                 final_norm_w.reshape(1, D), l == L - 1)
    return x
```
